```python
import math
import jax, jax.numpy as jnp
from jax import lax
import numpy as np

D_MODEL = 1024
BATCH = 2
SEQ = 8192
DEPTH = 4

CHUNK = 64
N_MIXERS = 3
EPS = 1e-6
GLA_HEADS = 4
GLA_DK = D_MODEL // (2 * GLA_HEADS)
GLA_DV = D_MODEL // GLA_HEADS
GLA_RANK = 16
GLA_TAU = 16.0
SSD_DINNER = 2 * D_MODEL
SSD_HEADDIM = 64
SSD_HEADS = SSD_DINNER // SSD_HEADDIM
SSD_GROUPS = 8
SSD_HPG = SSD_HEADS // SSD_GROUPS
SSD_DSTATE = 128
SSD_CONV = 4
S5_GROUP = 16
S5_GROUPS = D_MODEL // S5_GROUP
S5_STATE = 64
FFN_HIDDEN = ((-(-8 * D_MODEL // 3)) + 255) // 256 * 256

kernel_name = 'chunk_causal_hybrid_gla_ssd_s5'


def _layers_of(mixer):
    return len(range(mixer, DEPTH, N_MIXERS))


def rmsnorm(x, g):
    xf = x.astype(jnp.float32)
    y = xf * lax.rsqrt(jnp.mean(xf * xf, axis=-1, keepdims=True) + EPS)
    return (y * g.astype(jnp.float32)).astype(x.dtype)


def causal_depthwise_conv(x, w, b):
    k = w.shape[0]
    out = lax.conv_general_dilated(
        x, w[:, None, :].astype(x.dtype), window_strides=(1,), padding=[(k - 1, 0)],
        dimension_numbers=('NWC', 'WIO', 'NWC'), feature_group_count=x.shape[-1])
    return out + b.astype(x.dtype)


def gla_mixer(h, w_in, w_a2, b_a, norm_g, w_out):
    bsz, seq, _ = h.shape
    nc = seq // CHUNK
    f32 = jnp.float32
    qk = GLA_HEADS * GLA_DK
    vd = GLA_HEADS * GLA_DV
    proj = h @ w_in
    q, k, v, r, a_low = jnp.split(proj, [qk, 2 * qk, 2 * qk + vd, 2 * qk + 2 * vd], axis=-1)
    log_a = jax.nn.log_sigmoid((a_low @ w_a2 + b_a).astype(f32)) / GLA_TAU

    def chunks(t, d):
        return t.reshape(bsz, nc, CHUNK, GLA_HEADS, d).astype(f32)

    q = chunks(q, GLA_DK) * (GLA_DK ** -0.5)
    k = chunks(k, GLA_DK)
    v = chunks(v, GLA_DV)
    lc = jnp.cumsum(chunks(log_a, GLA_DK), axis=2)
    lend = lc[:, :, -1:]
    q_fwd = q * jnp.exp(lc)
    k_fwd = k * jnp.exp(-lc)
    q_bwd = q * jnp.exp(-lc)
    k_bwd = k * jnp.exp(lc)
    s_past = jnp.einsum('bclhd,bcshd->bchls', q_fwd, k_fwd)
    s_future = jnp.einsum('bclhd,bcshd->bchls', q_bwd, k_bwd)
    past_mask = jnp.tril(jnp.ones((CHUNK, CHUNK), dtype=bool))
    scores = jnp.where(past_mask, s_past, s_future)
    o = jnp.einsum('bchls,bcshv->bclhv', scores, v)
    g_chunk = jnp.exp(lend[:, :, 0])
    d_state = jnp.einsum('bcshd,bcshv->bchdv', k * jnp.exp(lend - lc), v)

    def step(s, inp):
        g, ds = inp
        return g[..., None] * s + ds, s

    s0 = jnp.zeros((bsz, GLA_HEADS, GLA_DK, GLA_DV), f32)
    _, s_prev = lax.scan(step, s0, (jnp.moveaxis(g_chunk, 1, 0), jnp.moveaxis(d_state, 1, 0)))
    s_prev = jnp.moveaxis(s_prev, 0, 1)
    o = o + jnp.einsum('bclhd,bchdv->bclhv', q_fwd, s_prev)
    o = rmsnorm(o.reshape(bsz, seq, GLA_HEADS, GLA_DV), norm_g.reshape(GLA_HEADS, GLA_DV))
    o = o.reshape(bsz, seq, vd) * jax.nn.silu(r.astype(f32))
    return (o @ w_out).astype(h.dtype)


def ssd_mixer(h, w_in, conv_w, conv_b, dt_bias, a_log, d_skip, norm_g, w_out):
    bsz, seq, _ = h.shape
    nc = seq // CHUNK
    f32 = jnp.float32
    gn = SSD_GROUPS * SSD_DSTATE
    proj = h @ w_in
    z, xbc, dt = jnp.split(proj, [SSD_DINNER, 2 * SSD_DINNER + 2 * gn], axis=-1)
    xbc = jax.nn.silu(causal_depthwise_conv(xbc, conv_w, conv_b))
    xs, bm, cm = jnp.split(xbc, [SSD_DINNER, SSD_DINNER + gn], axis=-1)
    dt = jax.nn.softplus(dt.astype(f32) + dt_bias.astype(f32))
    da = dt * (-jnp.exp(a_log.astype(f32)))
    xs = xs.reshape(bsz, nc, CHUNK, SSD_GROUPS, SSD_HPG, SSD_HEADDIM).astype(f32)
    bm = bm.reshape(bsz, nc, CHUNK, SSD_GROUPS, SSD_DSTATE).astype(f32)
    cm = cm.reshape(bsz, nc, CHUNK, SSD_GROUPS, SSD_DSTATE).astype(f32)
    dt = dt.reshape(bsz, nc, CHUNK, SSD_GROUPS, SSD_HPG)
    cum = jnp.cumsum(da.reshape(bsz, nc, CHUNK, SSD_GROUPS, SSD_HPG), axis=2)
    cb = jnp.einsum('bclgn,bcsgn->bcgls', cm, bm)
    cum_h = jnp.moveaxis(cum, 2, -1)
    decay = jnp.exp(-jnp.abs(cum_h[..., :, None] - cum_h[..., None, :]))
    dt_h = jnp.moveaxis(dt, 2, -1)
    mix = cb[:, :, :, None] * decay * dt_h[..., None, :]
    y = jnp.einsum('bcgjls,bcsgjp->bclgjp', mix, xs)
    cum_end = cum[:, :, -1]
    xw = xs * (dt * jnp.exp(cum_end[:, :, None] - cum))[..., None]
    states = jnp.einsum('bcsgn,bcsgjp->bcgjpn', bm, xw)

    def step(hs, inp):
        a, s = inp
        return a[..., None, None] * hs + s, hs

    h0 = jnp.zeros((bsz, SSD_GROUPS, SSD_HPG, SSD_HEADDIM, SSD_DSTATE), f32)
    _, h_prev = lax.scan(step, h0, (jnp.moveaxis(jnp.exp(cum_end), 1, 0), jnp.moveaxis(states, 1, 0)))
    h_prev = jnp.moveaxis(h_prev, 0, 1)
    y = y + jnp.einsum('bclgn,bcgjpn->bclgjp', cm, h_prev) * jnp.exp(cum)[..., None]
    y = y + d_skip.astype(f32).reshape(SSD_GROUPS, SSD_HPG)[:, :, None] * xs
    y = y.reshape(bsz, seq, SSD_DINNER) * jax.nn.silu(z.astype(f32))
    gsz = SSD_DINNER // SSD_GROUPS
    y = rmsnorm(y.reshape(bsz, seq, SSD_GROUPS, gsz), norm_g.reshape(SSD_GROUPS, gsz))
    return (y.reshape(bsz, seq, SSD_DINNER) @ w_out).astype(h.dtype)


def _complex_linear_combine(e1, e2):
    a1r, a1i, b1r, b1i = e1
    a2r, a2i, b2r, b2i = e2
    ar = a2r * a1r - a2i * a1i
    ai = a2r * a1i + a2i * a1r
    br = a2r * b1r - a2i * b1i + b2r
    bi = a2r * b1i + a2i * b1r + b2i
    return ar, ai, br, bi


def s5_mixer(h, log_dt, a_re, a_im, b_re, b_im, c_re, c_im, d_skip, w_glu):
    bsz, seq, _ = h.shape
    f32 = jnp.float32
    u = h.reshape(bsz, seq, S5_GROUPS, S5_GROUP).astype(f32)
    step = jnp.exp(log_dt.astype(f32))[:, None]
    a_re = a_re.astype(f32)
    a_im = a_im.astype(f32)
    mag = jnp.exp(step * a_re)
    abar_re = mag * jnp.cos(step * a_im)
    abar_im = mag * jnp.sin(step * a_im)
    den = a_re * a_re + a_im * a_im
    num_re = abar_re - 1.0
    num_im = abar_im
    f_re = (num_re * a_re + num_im * a_im) / den
    f_im = (num_im * a_re - num_re * a_im) / den
    b_re = b_re.astype(f32)
    b_im = b_im.astype(f32)
    bb_re = f_re[..., None] * b_re - f_im[..., None] * b_im
    bb_im = f_re[..., None] * b_im + f_im[..., None] * b_re
    bu_re = jnp.einsum('gpc,blgc->blgp', bb_re, u)
    bu_im = jnp.einsum('gpc,blgc->blgp', bb_im, u)
    a_seq_re = jnp.broadcast_to(abar_re, bu_re.shape)
    a_seq_im = jnp.broadcast_to(abar_im, bu_im.shape)
    _, _, x_re, x_im = lax.associative_scan(
        _complex_linear_combine, (a_seq_re, a_seq_im, bu_re, bu_im), axis=1)
    y = (jnp.einsum('gcp,blgp->blgc', c_re.astype(f32), x_re)
         - jnp.einsum('gcp,blgp->blgc', c_im.astype(f32), x_im))
    y = y + d_skip.astype(f32).reshape(S5_GROUPS, S5_GROUP) * u
    y = jax.nn.gelu(y.reshape(bsz, seq, D_MODEL))
    val, gate = jnp.split(y @ w_glu, 2, axis=-1)
    return (val * jax.nn.sigmoid(gate)).astype(h.dtype)


def swiglu_ffn(h, w_gu, w_down):
    g, u = jnp.split(h @ w_gu, 2, axis=-1)
    return (jax.nn.silu(g) * u) @ w_down


def setup_inputs(seed: int = 0) -> dict:
    key = jax.random.key(seed)
    ks = iter(jax.random.split(key, 48))
    f32 = jnp.float32

    def nrm(shape, scale):
        return jax.random.normal(next(ks), shape, f32) * scale

    n_gla, n_ssd, n_s5 = _layers_of(0), _layers_of(1), _layers_of(2)
    qk = GLA_HEADS * GLA_DK
    vd = GLA_HEADS * GLA_DV
    gla_in = 2 * qk + 2 * vd + GLA_RANK
    gn = SSD_GROUPS * SSD_DSTATE
    ssd_conv_dim = SSD_DINNER + 2 * gn
    ssd_in = SSD_DINNER + ssd_conv_dim + SSD_HEADS

    x = jax.random.normal(next(ks), (BATCH, SEQ, D_MODEL), f32)
    norm_mix_g = 1.0 + nrm((DEPTH, D_MODEL), 0.01)
    norm_ffn_g = 1.0 + nrm((DEPTH, D_MODEL), 0.01)
    gla_w_in = nrm((n_gla, D_MODEL, gla_in), D_MODEL ** -0.5)
    gla_w_a2 = nrm((n_gla, GLA_RANK, qk), GLA_RANK ** -0.5)
    gla_b_a = nrm((n_gla, qk), 0.1)
    gla_norm_g = 1.0 + nrm((n_gla, vd), 0.01)
    gla_w_out = nrm((n_gla, vd, D_MODEL), vd ** -0.5)
    ssd_w_in = nrm((n_ssd, D_MODEL, ssd_in), D_MODEL ** -0.5)
    ssd_conv_w = nrm((n_ssd, SSD_CONV, ssd_conv_dim), SSD_CONV ** -0.5)
    ssd_conv_b = nrm((n_ssd, ssd_conv_dim), 0.02)
    dt0 = jnp.exp(jax.random.uniform(next(ks), (n_ssd, SSD_HEADS), f32, math.log(1e-3), math.log(1e-1)))
    ssd_dt_bias = dt0 + jnp.log(-jnp.expm1(-dt0))
    ssd_a_log = jnp.log(jax.random.uniform(next(ks), (n_ssd, SSD_HEADS), f32, 1.0, 16.0))
    ssd_d = 1.0 + nrm((n_ssd, SSD_HEADS), 0.1)
    ssd_norm_g = 1.0 + nrm((n_ssd, SSD_DINNER), 0.01)
    ssd_w_out = nrm((n_ssd, SSD_DINNER, D_MODEL), SSD_DINNER ** -0.5)
    s5_log_dt = jax.random.uniform(next(ks), (n_s5, S5_GROUPS), f32, math.log(1e-3), math.log(1e-1))
    n_idx = jnp.arange(S5_STATE, dtype=f32)
    s5_a_re = -0.5 + nrm((n_s5, S5_GROUPS, S5_STATE), 0.01)
    s5_a_im = jnp.pi * n_idx + nrm((n_s5, S5_GROUPS, S5_STATE), 0.01)
    bscale = (2.0 * S5_GROUP) ** -0.5
    s5_b_re = nrm((n_s5, S5_GROUPS, S5_STATE, S5_GROUP), bscale)
    s5_b_im = nrm((n_s5, S5_GROUPS, S5_STATE, S5_GROUP), bscale)
    cscale = (2.0 * S5_STATE) ** -0.5
    s5_c_re = nrm((n_s5, S5_GROUPS, S5_GROUP, S5_STATE), cscale)
    s5_c_im = nrm((n_s5, S5_GROUPS, S5_GROUP, S5_STATE), cscale)
    s5_d = 1.0 + nrm((n_s5, D_MODEL), 0.1)
    s5_w_glu = nrm((n_s5, D_MODEL, 2 * D_MODEL), D_MODEL ** -0.5)
    ffn_w_gu = nrm((DEPTH, D_MODEL, 2 * FFN_HIDDEN), D_MODEL ** -0.5)
    ffn_w_down = nrm((DEPTH, FFN_HIDDEN, D_MODEL), FFN_HIDDEN ** -0.5)
    final_norm_g = 1.0 + nrm((D_MODEL,), 0.01)
    return {
        'x': x, 'norm_mix_g': norm_mix_g, 'norm_ffn_g': norm_ffn_g,
        'gla_w_in': gla_w_in, 'gla_w_a2': gla_w_a2, 'gla_b_a': gla_b_a,
        'gla_norm_g': gla_norm_g, 'gla_w_out': gla_w_out,
        'ssd_w_in': ssd_w_in, 'ssd_conv_w': ssd_conv_w, 'ssd_conv_b': ssd_conv_b,
        'ssd_dt_bias': ssd_dt_bias, 'ssd_a_log': ssd_a_log, 'ssd_d': ssd_d,
        'ssd_norm_g': ssd_norm_g, 'ssd_w_out': ssd_w_out,
        's5_log_dt': s5_log_dt, 's5_a_re': s5_a_re, 's5_a_im': s5_a_im,
        's5_b_re': s5_b_re, 's5_b_im': s5_b_im, 's5_c_re': s5_c_re, 's5_c_im': s5_c_im,
        's5_d': s5_d, 's5_w_glu': s5_w_glu,
        'ffn_w_gu': ffn_w_gu, 'ffn_w_down': ffn_w_down, 'final_norm_g': final_norm_g,
    }


def reference(x, norm_mix_g, norm_ffn_g,
              gla_w_in, gla_w_a2, gla_b_a, gla_norm_g, gla_w_out,
              ssd_w_in, ssd_conv_w, ssd_conv_b, ssd_dt_bias, ssd_a_log, ssd_d, ssd_norm_g, ssd_w_out,
              s5_log_dt, s5_a_re, s5_a_im, s5_b_re, s5_b_im, s5_c_re, s5_c_im, s5_d, s5_w_glu,
              ffn_w_gu, ffn_w_down, final_norm_g):
    h = x
    for i in range(DEPTH):
        mixer, j = i % N_MIXERS, i // N_MIXERS
        hn = rmsnorm(h, norm_mix_g[i])
        if mixer == 0:
            y = gla_mixer(hn, gla_w_in[j], gla_w_a2[j], gla_b_a[j], gla_norm_g[j], gla_w_out[j])
        elif mixer == 1:
            y = ssd_mixer(hn, ssd_w_in[j], ssd_conv_w[j], ssd_conv_b[j], ssd_dt_bias[j],
                          ssd_a_log[j], ssd_d[j], ssd_norm_g[j], ssd_w_out[j])
        else:
            y = s5_mixer(hn, s5_log_dt[j], s5_a_re[j], s5_a_im[j], s5_b_re[j], s5_b_im[j],
                         s5_c_re[j], s5_c_im[j], s5_d[j], s5_w_glu[j])
        h = h + y.astype(h.dtype)
        h = h + swiglu_ffn(rmsnorm(h, norm_ffn_g[i]), ffn_w_gu[i], ffn_w_down[i]).astype(h.dtype)
    return rmsnorm(h, final_norm_g)
```

```python
import functools
import math

import jax
import jax.numpy as jnp
from jax import lax
from jax.experimental import pallas as pl
from jax.experimental.pallas import tpu as pltpu

F32 = jnp.float32
BF16 = jnp.bfloat16

EPS = 1e-6
CHUNK = 64
SUB = 2 * CHUNK
LANES = 128
SUBLANES = 8

GLA_HEADS = 4
GLA_DK = 128
GLA_DV = 256
GLA_RANK = 16
GLA_TAU = 16.0
SSD_HEADDIM = 64
SSD_GROUPS = 8
SSD_HPG = 4
SSD_DSTATE = 128
SSD_CONV = 4
S5_GROUP = 16
S5_STATE = 64
S5_GPB = LANES // S5_GROUP

VMEM_LIMIT = 56 * 1024 * 1024


def _dot(a, b):
    return jnp.dot(a, b, preferred_element_type=F32)


def _dot_nt(a, b):
    return lax.dot_general(a, b, (((1,), (1,)), ((), ())), preferred_element_type=F32)


def _dot_tn(a, b):
    return lax.dot_general(a, b, (((0,), (0,)), ((), ())), preferred_element_type=F32)


def _split3(x):
    h1 = x.astype(BF16)
    r1 = x - h1.astype(F32)
    h2 = r1.astype(BF16)
    r2 = r1 - h2.astype(F32)
    return h1, h2, r2.astype(BF16)


def _dot_exact_lhs(t, x):
    h1, h2, h3 = _split3(x)
    return _dot(t, h1) + _dot(t, h2) + _dot(t, h3)


def _rms(x, g):
    ms = jnp.mean(x * x, axis=-1, keepdims=True)
    return x * lax.rsqrt(ms + EPS) * g


def _sigmoid(x):
    return 1.0 / (1.0 + jnp.exp(-x))


def _softplus(x):
    return jnp.maximum(x, 0.0) + jnp.log1p(jnp.exp(-jnp.abs(x)))


def _const_spec(shape):
    nd = len(shape)
    return pl.BlockSpec(shape, lambda b, j: (0,) * nd, pipeline_mode=pl.Buffered(1))


def _params():
    return pltpu.CompilerParams(
        dimension_semantics=("arbitrary", "arbitrary"),
        vmem_limit_bytes=VMEM_LIMIT)


def _ffn_kernel(h_ref, ng_ref, wgu_ref, wd_ref, fg_ref, o_ref, *, hidden, final):
    x = h_ref[0]
    xn = _rms(x, ng_ref[...]).astype(BF16)
    gu = _dot(xn, wgu_ref[...])
    g = gu[:, :hidden]
    u = gu[:, hidden:]
    a = (g * _sigmoid(g) * u).astype(BF16)
    out = x + _dot(a, wd_ref[...])
    if final:
        out = _rms(out, fg_ref[...])
    o_ref[0] = out


def _ffn_layer(h, norm_g, w_gu, w_down, final_g, *, tm=512):
    bsz, seq, d = h.shape
    hidden = w_down.shape[0]
    tm = min(tm, seq)
    final = final_g is not None
    fg = (final_g if final else norm_g).reshape(1, d)
    kern = functools.partial(_ffn_kernel, hidden=hidden, final=final)
    return pl.pallas_call(
        kern,
        grid=(bsz, seq // tm),
        in_specs=[
            pl.BlockSpec((1, tm, d), lambda b, j: (b, j, 0)),
            _const_spec((1, d)),
            _const_spec((d, 2 * hidden)),
            _const_spec((hidden, d)),
            _const_spec((1, d)),
        ],
        out_specs=pl.BlockSpec((1, tm, d), lambda b, j: (b, j, 0)),
        out_shape=jax.ShapeDtypeStruct(h.shape, F32),
        compiler_params=_params(),
        name="swiglu_ffn",
    )(h, norm_g.reshape(1, d), w_gu.astype(BF16), w_down.astype(BF16), fg)


def _two_chunk_masks():
    ti = lax.broadcasted_iota(jnp.int32, (SUB, SUB), 0)
    si = lax.broadcasted_iota(jnp.int32, (SUB, SUB), 1)
    same = (ti < CHUNK) == (si < CHUNK)
    return ti, si, same


def _gla_kernel(h_ref, ng_ref, wm_ref, wa1_ref, wa2_ref, ba_ref, og_ref, wo_ref, o_ref,
                proj_s, la_s, gated_s, state_s, *, tm):
    qk = GLA_HEADS * GLA_DK
    vd = GLA_HEADS * GLA_DV

    @pl.when(pl.program_id(1) == 0)
    def _():
        state_s[...] = jnp.zeros_like(state_s)

    x = h_ref[0]
    hn = _rms(x, ng_ref[...]).astype(BF16)
    proj_s[...] = _dot(hn, wm_ref[...])
    a_low = _dot(hn, wa1_ref[...]).astype(BF16)
    a = _dot(a_low, wa2_ref[...]) + ba_ref[...]
    la_s[...] = (jnp.minimum(a, 0.0) - jnp.log1p(jnp.exp(-jnp.abs(a)))) * (1.0 / GLA_TAU)

    ti, si, same = _two_chunk_masks()
    tri = jnp.where(same & (si <= ti), 1.0, 0.0).astype(BF16)
    code = jnp.where(same, jnp.where(si <= ti, 0, 1), jnp.where(si < ti, 2, 3))
    first = lax.broadcasted_iota(jnp.int32, (SUB, 1), 0) < CHUNK
    og = og_ref[...]

    def step(i, carry):
        rows = pl.ds(pl.multiple_of(i * SUB, SUB), SUB)
        lc = _dot_exact_lhs(tri, la_s[rows, :])
        lend0 = lc[CHUNK - 1:CHUNK, :]
        lend1 = lc[SUB - 1:SUB, :]
        lend = jnp.where(first, lend0, lend1)
        e_pos = jnp.exp(lc)
        e_neg = jnp.exp(-lc)
        e_st = jnp.exp(lend - lc)
        g0 = jnp.exp(lend0)
        g1 = jnp.exp(lend1)
        q = proj_s[rows, 0:qk] * (GLA_DK ** -0.5)
        k = proj_s[rows, qk:2 * qk]
        q_fwd = q * e_pos
        q_bwd = (q * e_neg).astype(BF16)
        k_fwd = (k * e_neg).astype(BF16)
        k_bwd = (k * e_pos).astype(BF16)
        k_st = k * e_st
        q_t = (q_fwd * jnp.where(first, 1.0, g0)).astype(BF16)
        k_t = (k_st * jnp.where(first, g1, 1.0)).astype(BF16)
        q_fwd = q_fwd.astype(BF16)
        k_st = k_st.astype(BF16)
        g_all = g0 * g1
        for hd in range(GLA_HEADS):
            ks = slice(hd * GLA_DK, (hd + 1) * GLA_DK)
            vs = slice(hd * GLA_DV, (hd + 1) * GLA_DV)
            s_past = _dot_nt(q_fwd[:, ks], k_fwd[:, ks])
            s_future = _dot_nt(q_bwd[:, ks], k_bwd[:, ks])
            s_cross = _dot_nt(q_fwd[:, ks], k_st[:, ks])
            p = jnp.where(code == 0, s_past,
                          jnp.where(code == 1, s_future,
                                    jnp.where(code == 2, s_cross, 0.0))).astype(BF16)
            v = proj_s[rows, 2 * qk + hd * GLA_DV:2 * qk + (hd + 1) * GLA_DV].astype(BF16)
            st = state_s[hd]
            o = _dot(p, v) + _dot(q_t[:, ks], st.astype(BF16))
            g_col = jnp.broadcast_to(g_all[:, ks], (GLA_DK, GLA_DK)).T
            state_s[hd] = st * jnp.concatenate([g_col, g_col], axis=1) + _dot_tn(k_t[:, ks], v)
            ms = jnp.mean(o * o, axis=-1, keepdims=True)
            on = o * lax.rsqrt(ms + EPS) * og[:, vs]
            r = proj_s[rows, 2 * qk + vd + hd * GLA_DV:2 * qk + vd + (hd + 1) * GLA_DV]
            gated_s[rows, vs] = (on * (r * _sigmoid(r))).astype(BF16)
        return carry

    lax.fori_loop(0, tm // SUB, step, 0)
    o_ref[0] = x + _dot(gated_s[...], wo_ref[...])


def _gla_layer(h, norm_g, w_in, w_a2, b_a, out_g, w_out, *, tm=256):
    bsz, seq, d = h.shape
    qk = GLA_HEADS * GLA_DK
    vd = GLA_HEADS * GLA_DV
    nmain = 2 * qk + 2 * vd
    tm = min(tm, seq)
    w_main = w_in[:, :nmain].astype(BF16)
    w_a1 = jnp.pad(w_in[:, nmain:], ((0, 0), (0, LANES - GLA_RANK))).astype(BF16)
    w_a2p = jnp.pad(w_a2, ((0, LANES - GLA_RANK), (0, 0))).astype(BF16)
    kern = functools.partial(_gla_kernel, tm=tm)
    return pl.pallas_call(
        kern,
        grid=(bsz, seq // tm),
        in_specs=[
            pl.BlockSpec((1, tm, d), lambda b, j: (b, j, 0)),
            _const_spec((1, d)),
            _const_spec((d, nmain)),
            _const_spec((d, LANES)),
            _const_spec((LANES, qk)),
            _const_spec((1, qk)),
            _const_spec((1, vd)),
            _const_spec((vd, d)),
        ],
        out_specs=pl.BlockSpec((1, tm, d), lambda b, j: (b, j, 0)),
        out_shape=jax.ShapeDtypeStruct(h.shape, F32),
        scratch_shapes=[
            pltpu.VMEM((tm, nmain), F32),
            pltpu.VMEM((tm, qk), F32),
            pltpu.VMEM((tm, vd), BF16),
            pltpu.VMEM((GLA_HEADS, GLA_DK, GLA_DV), F32),
        ],
        compiler_params=_params(),
        name="gla_mixer",
    )(h, norm_g.reshape(1, d), w_main, w_a1, w_a2p, b_a.reshape(1, qk),
      out_g.reshape(1, vd), w_out.astype(BF16))


def _pack3(x):
    h1, h2, h3 = _split3(x)
    lane = lax.broadcasted_iota(jnp.int32, x.shape, 1)
    zero = jnp.zeros_like(h1)
    return jnp.where(lane < 32, h1, jnp.where(lane < 64, h2, jnp.where(lane < 96, h3, zero)))


def _ssd_kernel(h_ref, ng_ref, wz_ref, wx_ref, wdt_ref, cw_ref, cb_ref, dtb_ref, alog_ref,
                dsk_ref, og_ref, wo_ref, ex_ref, o_ref,
                ext_s, z_s, xbc_s, dt_s, da_s, y_s, ht_s, *, tm):
    dinner = SSD_GROUPS * SSD_HPG * SSD_HEADDIM
    gn = SSD_GROUPS * SSD_DSTATE
    gw = SSD_HPG * SSD_HEADDIM

    @pl.when(pl.program_id(1) == 0)
    def _():
        ht_s[...] = jnp.zeros_like(ht_s)
        ext_s[0:SUBLANES, :] = jnp.zeros((SUBLANES, ext_s.shape[1]), F32)

    x = h_ref[0]
    hn = _rms(x, ng_ref[...]).astype(BF16)
    z_s[...] = _dot(hn, wz_ref[...])
    ext_s[SUBLANES:SUBLANES + tm, :] = _dot(hn, wx_ref[...])
    dt = _softplus(_dot(hn, wdt_ref[...]) + dtb_ref[...])
    dt_s[...] = dt
    da_s[...] = dt * (-jnp.exp(alog_ref[...]))

    cblk = 512
    for c in range(ext_s.shape[1] // cblk):
        cs = slice(c * cblk, (c + 1) * cblk)
        acc = cb_ref[:, cs] + cw_ref[SSD_CONV - 1:SSD_CONV, cs] * ext_s[SUBLANES:SUBLANES + tm, cs]
        for kk in range(SSD_CONV - 1):
            off = SUBLANES - (SSD_CONV - 1) + kk
            acc = acc + cw_ref[kk:kk + 1, cs] * ext_s[off:off + tm, cs]
        xbc_s[:, cs] = acc * _sigmoid(acc)
    ext_s[0:SUBLANES, :] = ext_s[tm:tm + SUBLANES, :]

    ti = lax.broadcasted_iota(jnp.int32, (SUB, SUB), 0)
    si = lax.broadcasted_iota(jnp.int32, (SUB, SUB), 1)
    tri = jnp.where(si <= ti, 1.0, 0.0).astype(BF16)
    visible = (si < CHUNK) | (ti >= CHUNK)
    og = og_ref[...]
    dsk = dsk_ref[...]
    expand = ex_ref[...]

    def step(i, carry):
        rows = pl.ds(pl.multiple_of(i * SUB, SUB), SUB)
        dts = dt_s[rows, :]
        cum = _dot_exact_lhs(tri, da_s[rows, :])
        cum_end = cum[SUB - 1:SUB, :]
        dec_e = _dot(_pack3(jnp.exp(cum)), expand)
        dec_w = _dot(_pack3(dts * jnp.exp(cum_end - cum)), expand)
        cum_t = cum.T
        dt_t = dts.T
        xs = xbc_s[rows, 0:dinner]
        xs_b = xs.astype(BF16)
        xw = (xs * dec_w).astype(BF16)
        for g in range(SSD_GROUPS):
            gs = slice(g * gw, (g + 1) * gw)
            bm = xbc_s[rows, dinner + g * SSD_DSTATE:dinner + (g + 1) * SSD_DSTATE].astype(BF16)
            cm = xbc_s[rows, dinner + gn + g * SSD_DSTATE:dinner + gn + (g + 1) * SSD_DSTATE].astype(BF16)
            cb = _dot_nt(cm, bm)
            ht = ht_s[g]
            y_state = _dot(cm, ht.astype(BF16)) * dec_e[:, gs]
            ys = []
            for jj in range(SSD_HPG):
                hd = g * SSD_HPG + jj
                diff = cum[:, hd:hd + 1] - cum_t[hd:hd + 1, :]
                mix = cb * jnp.exp(-jnp.abs(diff)) * dt_t[hd:hd + 1, :]
                mix = jnp.where(visible, mix, 0.0).astype(BF16)
                ys.append(_dot(mix, xs_b[:, hd * SSD_HEADDIM:(hd + 1) * SSD_HEADDIM]))
            y = jnp.concatenate(ys, axis=1) + y_state + dsk[:, gs] * xs[:, gs]
            ht_s[g] = ht * dec_e[SUB - 1:SUB, gs] + _dot_tn(bm, xw[:, gs])
            zg = z_s[rows, gs]
            y = y * (zg * _sigmoid(zg))
            ms = jnp.mean(y * y, axis=-1, keepdims=True)
            y_s[rows, gs] = (y * lax.rsqrt(ms + EPS) * og[:, gs]).astype(BF16)
        return carry

    lax.fori_loop(0, tm // SUB, step, 0)
    o_ref[0] = x + _dot(y_s[...], wo_ref[...])


def _ssd_layer(h, norm_g, w_in, conv_w, conv_b, dt_bias, a_log, d_skip, out_g, w_out, *, tm=256):
    bsz, seq, d = h.shape
    heads = SSD_GROUPS * SSD_HPG
    dinner = heads * SSD_HEADDIM
    gn = SSD_GROUPS * SSD_DSTATE
    cdim = dinner + 2 * gn
    tm = min(tm, seq)

    def rep3(v):
        pad = [(0, 0)] * (v.ndim - 1) + [(0, LANES - 3 * heads)]
        return jnp.pad(jnp.concatenate([v, v, v], axis=-1), pad)

    w_z = w_in[:, :dinner].astype(BF16)
    w_x = w_in[:, dinner:dinner + cdim].astype(BF16)
    w_dt = rep3(w_in[:, dinner + cdim:]).astype(BF16)
    lane = jnp.arange(LANES)[:, None]
    col = jnp.arange(dinner)[None, :]
    expand = ((lane < 3 * heads) & ((lane % heads) == (col // SSD_HEADDIM))).astype(BF16)
    kern = functools.partial(_ssd_kernel, tm=tm)
    return pl.pallas_call(
        kern,
        grid=(bsz, seq // tm),
        in_specs=[
            pl.BlockSpec((1, tm, d), lambda b, j: (b, j, 0)),
            _const_spec((1, d)),
            _const_spec((d, dinner)),
            _const_spec((d, cdim)),
            _const_spec((d, LANES)),
            _const_spec((SSD_CONV, cdim)),
            _const_spec((1, cdim)),
            _const_spec((1, LANES)),
            _const_spec((1, LANES)),
            _const_spec((1, dinner)),
            _const_spec((1, dinner)),
            _const_spec((dinner, d)),
            _const_spec((LANES, dinner)),
        ],
        out_specs=pl.BlockSpec((1, tm, d), lambda b, j: (b, j, 0)),
        out_shape=jax.ShapeDtypeStruct(h.shape, F32),
        scratch_shapes=[
            pltpu.VMEM((tm + SUBLANES, cdim), F32),
            pltpu.VMEM((tm, dinner), F32),
            pltpu.VMEM((tm, cdim), F32),
            pltpu.VMEM((tm, LANES), F32),
            pltpu.VMEM((tm, LANES), F32),
            pltpu.VMEM((tm, dinner), BF16),
            pltpu.VMEM((SSD_GROUPS, SSD_DSTATE, SSD_HPG * SSD_HEADDIM), F32),
        ],
        compiler_params=_params(),
        name="ssd_mixer",
    )(h, norm_g.reshape(1, d), w_z, w_x, w_dt, conv_w, conv_b.reshape(1, cdim),
      rep3(dt_bias).reshape(1, LANES), rep3(a_log).reshape(1, LANES),
      jnp.repeat(d_skip, SSD_HEADDIM).reshape(1, dinner), out_g.reshape(1, dinner),
      w_out.astype(BF16), expand)


def _s5_kernel(h_ref, ng_ref, wb_ref, wc_ref, tab_ref, dsk_ref, wg_ref, o_ref,
               xr_s, xi_s, y_s, car_s, *, tm):
    d = h_ref.shape[2]
    nblk = d // LANES
    sw = S5_GPB * S5_STATE

    @pl.when(pl.program_id(1) == 0)
    def _():
        car_s[...] = jnp.zeros_like(car_s)

    x = h_ref[0]
    u = _rms(x, ng_ref[...])
    u_b = u.astype(BF16)

    for blk in range(nblk):
        bu = _dot(u_b[:, blk * LANES:(blk + 1) * LANES], wb_ref[blk])
        xr_s[...] = bu[:, :sw]
        xi_s[...] = bu[:, sw:]
        tabs = [tab_ref[blk, t] for t in range(8)]

        def scan_rows(kb, carry, tabs=tabs):
            cr, ci = carry
            rows = pl.ds(pl.multiple_of(kb * SUBLANES, SUBLANES), SUBLANES)
            xr = xr_s[rows, :]
            xi = xi_s[rows, :]
            for n, sh in enumerate((1, 2, 4)):
                ar, ai = tabs[2 * n], tabs[2 * n + 1]
                rr = pltpu.roll(xr, sh, 0)
                ri = pltpu.roll(xi, sh, 0)
                xr, xi = xr + ar * rr - ai * ri, xi + ar * ri + ai * rr
            pr, pi = tabs[6], tabs[7]
            cbr = jnp.broadcast_to(cr, xr.shape)
            cbi = jnp.broadcast_to(ci, xi.shape)
            xr, xi = xr + pr * cbr - pi * cbi, xi + pr * cbi + pi * cbr
            xr_s[rows, :] = xr
            xi_s[rows, :] = xi
            return xr[SUBLANES - 1:SUBLANES, :], xi[SUBLANES - 1:SUBLANES, :]

        cr, ci = lax.fori_loop(0, tm // SUBLANES, scan_rows,
                               (car_s[blk, 0:1, :], car_s[blk, 1:2, :]))
        car_s[blk, 0:1, :] = cr
        car_s[blk, 1:2, :] = ci
        xs = jnp.concatenate([xr_s[...].astype(BF16), xi_s[...].astype(BF16)], axis=1)
        y_s[:, blk * LANES:(blk + 1) * LANES] = _dot(xs, wc_ref[blk])

    y = y_s[...] + dsk_ref[...] * u
    c0 = math.sqrt(2.0 / math.pi)
    y = y * (0.5 * (1.0 + jnp.tanh(c0 * (y + 0.044715 * (y * y * y)))))
    vg = _dot(y.astype(BF16), wg_ref[...])
    o_ref[0] = x + vg[:, :d] * _sigmoid(vg[:, d:])


def _s5_tables(log_dt, a_re, a_im, b_re, b_im, c_re, c_im):
    groups, nstate = a_re.shape
    nblk = groups // S5_GPB
    sw = S5_GPB * S5_STATE
    step = jnp.exp(log_dt.astype(F32))[:, None]
    a_re = a_re.astype(F32)
    a_im = a_im.astype(F32)
    mag = jnp.exp(step * a_re)
    abar_re = mag * jnp.cos(step * a_im)
    abar_im = mag * jnp.sin(step * a_im)
    den = a_re * a_re + a_im * a_im
    num_re = abar_re - 1.0
    num_im = abar_im
    f_re = (num_re * a_re + num_im * a_im) / den
    f_im = (num_im * a_re - num_re * a_im) / den
    b_re = b_re.astype(F32)
    b_im = b_im.astype(F32)
    bb_re = f_re[..., None] * b_re - f_im[..., None] * b_im
    bb_im = f_re[..., None] * b_im + f_im[..., None] * b_re
    eye = jnp.eye(S5_GPB, dtype=F32)

    def in_mat(bb):
        t = bb.reshape(nblk, S5_GPB, nstate, S5_GROUP)
        m = jnp.einsum('bgpc,gh->bgchp', t, eye)
        return m.reshape(nblk, LANES, sw)

    def out_mat(cc):
        t = cc.reshape(nblk, S5_GPB, S5_GROUP, nstate)
        m = jnp.einsum('bgcp,gh->bgphc', t, eye)
        return m.reshape(nblk, sw, LANES)

    w_b = jnp.concatenate([in_mat(bb_re), in_mat(bb_im)], axis=2).astype(BF16)
    w_c = jnp.concatenate([out_mat(c_re.astype(F32)), -out_mat(c_im.astype(F32))], axis=1).astype(BF16)

    def power(k):
        m = jnp.exp(k * step * a_re)
        return m * jnp.cos(k * step * a_im), m * jnp.sin(k * step * a_im)

    row = jnp.arange(SUBLANES, dtype=F32)
    tabs = []
    for sh in (1, 2, 4):
        pr, pi = power(float(sh))
        keep = (row >= sh).astype(F32)[:, None, None]
        tabs += [keep * pr[None], keep * pi[None]]
    pr, pi = power((row + 1.0)[:, None, None])
    tabs += [pr, pi]
    tab = jnp.stack(tabs, axis=0)
    tab = tab.reshape(8, SUBLANES, nblk, sw).transpose(2, 0, 1, 3)
    return w_b, w_c, tab


def _s5_layer(h, norm_g, log_dt, a_re, a_im, b_re, b_im, c_re, c_im, d_skip, w_glu, *, tm=256):
    bsz, seq, d = h.shape
    nblk = d // LANES
    sw = S5_GPB * S5_STATE
    tm = min(tm, seq)
    w_b, w_c, tab = _s5_tables(log_dt, a_re, a_im, b_re, b_im, c_re, c_im)
    kern = functools.partial(_s5_kernel, tm=tm)
    return pl.pallas_call(
        kern,
        grid=(bsz, seq // tm),
        in_specs=[
            pl.BlockSpec((1, tm, d), lambda b, j: (b, j, 0)),
            _const_spec((1, d)),
            _const_spec((nblk, LANES, 2 * sw)),
            _const_spec((nblk, 2 * sw, LANES)),
            _const_spec((nblk, 8, SUBLANES, sw)),
            _const_spec((1, d)),
            _const_spec((d, 2 * d)),
        ],
        out_specs=pl.BlockSpec((1, tm, d), lambda b, j: (b, j, 0)),
        out_shape=jax.ShapeDtypeStruct(h.shape, F32),
        scratch_shapes=[
            pltpu.VMEM((tm, sw), F32),
            pltpu.VMEM((tm, sw), F32),
            pltpu.VMEM((tm, d), F32),
            pltpu.VMEM((nblk, SUBLANES, sw), F32),
        ],
        compiler_params=_params(),
        name="s5_mixer",
    )(h, norm_g.reshape(1, d), w_b, w_c, tab, d_skip.astype(F32).reshape(1, d), w_glu.astype(BF16))


def kernel(x, norm_mix_g, norm_ffn_g, gla_w_in, gla_w_a2, gla_b_a, gla_norm_g, gla_w_out, ssd_w_in, ssd_conv_w, ssd_conv_b, ssd_dt_bias, ssd_a_log, ssd_d, ssd_norm_g, ssd_w_out, s5_log_dt, s5_a_re, s5_a_im, s5_b_re, s5_b_im, s5_c_re, s5_c_im, s5_d, s5_w_glu, ffn_w_gu, ffn_w_down, final_norm_g):
    depth = norm_mix_g.shape[0]
    n_mixers = 3
    h = x
    for i in range(depth):
        mixer, j = i % n_mixers, i // n_mixers
        if mixer == 0:
            h = _gla_layer(h, norm_mix_g[i], gla_w_in[j], gla_w_a2[j], gla_b_a[j],
                           gla_norm_g[j], gla_w_out[j])
        elif mixer == 1:
            h = _ssd_layer(h, norm_mix_g[i], ssd_w_in[j], ssd_conv_w[j], ssd_conv_b[j],
                           ssd_dt_bias[j], ssd_a_log[j], ssd_d[j], ssd_norm_g[j], ssd_w_out[j])
        else:
            h = _s5_layer(h, norm_mix_g[i], s5_log_dt[j], s5_a_re[j], s5_a_im[j], s5_b_re[j],
                          s5_b_im[j], s5_c_re[j], s5_c_im[j], s5_d[j], s5_w_glu[j])
        h = _ffn_layer(h, norm_ffn_g[i], ffn_w_gu[i], ffn_w_down[i],
                       final_norm_g if i == depth - 1 else None)
    return h
```

```python
import functools
import math

import jax
import jax.numpy as jnp
from jax import lax
from jax.experimental import pallas as pl
from jax.experimental.pallas import tpu as pltpu

F32 = jnp.float32
BF16 = jnp.bfloat16

EPS = 1e-6
CHUNK = 64
SUB = 2 * CHUNK
LANES = 128
SUBLANES = 8

GLA_HEADS = 4
GLA_DK = 128
GLA_DV = 256
GLA_RANK = 16
GLA_TAU = 16.0
SSD_HEADDIM = 64
SSD_GROUPS = 8
SSD_HPG = 4
SSD_DSTATE = 128
SSD_CONV = 4
S5_GROUP = 16
S5_STATE = 64
S5_GPB = LANES // S5_GROUP

VMEM_LIMIT = 56 * 1024 * 1024


def _dot(a, b):
    return jnp.dot(a, b, preferred_element_type=F32)


def _dot_nt(a, b):
    return lax.dot_general(a, b, (((1,), (1,)), ((), ())), preferred_element_type=F32)


def _dot_tn(a, b):
    return lax.dot_general(a, b, (((0,), (0,)), ((), ())), preferred_element_type=F32)


def _split3(x):
    h1 = x.astype(BF16)
    r1 = x - h1.astype(F32)
    h2 = r1.astype(BF16)
    r2 = r1 - h2.astype(F32)
    return h1, h2, r2.astype(BF16)


def _dot_exact_lhs(t, x):
    h1, h2, h3 = _split3(x)
    return _dot(t, h1) + _dot(t, h2) + _dot(t, h3)


def _rms(x, g):
    ms = jnp.mean(x * x, axis=-1, keepdims=True)
    return x * lax.rsqrt(ms + EPS) * g


def _sigmoid(x):
    return 1.0 / (1.0 + jnp.exp(-x))


def _softplus(x):
    return jnp.maximum(x, 0.0) + jnp.log1p(jnp.exp(-jnp.abs(x)))


def _const_spec(shape):
    nd = len(shape)
    return pl.BlockSpec(shape, lambda b, j: (0,) * nd, pipeline_mode=pl.Buffered(1))


def _params():
    return pltpu.CompilerParams(
        dimension_semantics=("arbitrary", "arbitrary"),
        vmem_limit_bytes=VMEM_LIMIT)


def _ffn_kernel(h_ref, ng_ref, wgu_ref, wd_ref, fg_ref, o_ref, *, hidden, final):
    x = h_ref[0]
    xn = _rms(x, ng_ref[...]).astype(BF16)
    gu = _dot(xn, wgu_ref[...])
    g = gu[:, :hidden]
    u = gu[:, hidden:]
    a = (g * _sigmoid(g) * u).astype(BF16)
    out = x + _dot(a, wd_ref[...])
    if final:
        out = _rms(out, fg_ref[...])
    o_ref[0] = out


def _ffn_layer(h, norm_g, w_gu, w_down, final_g, *, tm=512):
    bsz, seq, d = h.shape
    hidden = w_down.shape[0]
    tm = min(tm, seq)
    final = final_g is not None
    fg = (final_g if final else norm_g).reshape(1, d)
    kern = functools.partial(_ffn_kernel, hidden=hidden, final=final)
    return pl.pallas_call(
        kern,
        grid=(bsz, seq // tm),
        in_specs=[
            pl.BlockSpec((1, tm, d), lambda b, j: (b, j, 0)),
            _const_spec((1, d)),
            _const_spec((d, 2 * hidden)),
            _const_spec((hidden, d)),
            _const_spec((1, d)),
        ],
        out_specs=pl.BlockSpec((1, tm, d), lambda b, j: (b, j, 0)),
        out_shape=jax.ShapeDtypeStruct(h.shape, F32),
        compiler_params=_params(),
        name="swiglu_ffn",
    )(h, norm_g.reshape(1, d), w_gu.astype(BF16), w_down.astype(BF16), fg)


def _two_chunk_masks():
    ti = lax.broadcasted_iota(jnp.int32, (SUB, SUB), 0)
    si = lax.broadcasted_iota(jnp.int32, (SUB, SUB), 1)
    same = (ti < CHUNK) == (si < CHUNK)
    return ti, si, same


def _gla_kernel(h_ref, ng_ref, wm_ref, wa1_ref, wa2_ref, ba_ref, og_ref, wo_ref, o_ref,
                proj_s, la_s, gated_s, state_s, *, tm):
    qk = GLA_HEADS * GLA_DK
    vd = GLA_HEADS * GLA_DV

    @pl.when(pl.program_id(1) == 0)
    def _():
        state_s[...] = jnp.zeros_like(state_s)

    x = h_ref[0]
    hn = _rms(x, ng_ref[...]).astype(BF16)
    proj_s[...] = _dot(hn, wm_ref[...])
    a_low = _dot(hn, wa1_ref[...]).astype(BF16)
    a = _dot(a_low, wa2_ref[...]) + ba_ref[...]
    la_s[...] = (jnp.minimum(a, 0.0) - jnp.log1p(jnp.exp(-jnp.abs(a)))) * (1.0 / GLA_TAU)

    ti, si, same = _two_chunk_masks()
    tri = jnp.where(same & (si <= ti), 1.0, 0.0).astype(BF16)
    code = jnp.where(same, jnp.where(si <= ti, 0, 1), jnp.where(si < ti, 2, 3))
    first = lax.broadcasted_iota(jnp.int32, (SUB, 1), 0) < CHUNK
    og = og_ref[...]

    def step(i, carry):
        rows = pl.ds(pl.multiple_of(i * SUB, SUB), SUB)
        lc = _dot_exact_lhs(tri, la_s[rows, :])
        lend0 = lc[CHUNK - 1:CHUNK, :]
        lend1 = lc[SUB - 1:SUB, :]
        lend = jnp.where(first, lend0, lend1)
        e_pos = jnp.exp(lc)
        e_neg = jnp.exp(-lc)
        e_st = jnp.exp(lend - lc)
        g0 = jnp.exp(lend0)
        g1 = jnp.exp(lend1)
        q = proj_s[rows, 0:qk] * (GLA_DK ** -0.5)
        k = proj_s[rows, qk:2 * qk]
        q_fwd = q * e_pos
        q_bwd = (q * e_neg).astype(BF16)
        k_fwd = (k * e_neg).astype(BF16)
        k_bwd = (k * e_pos).astype(BF16)
        k_st = k * e_st
        q_t = (q_fwd * jnp.where(first, 1.0, g0)).astype(BF16)
        k_t = (k_st * jnp.where(first, g1, 1.0)).astype(BF16)
        q_fwd = q_fwd.astype(BF16)
        k_st = k_st.astype(BF16)
        g_all = g0 * g1
        for hd in range(GLA_HEADS):
            ks = slice(hd * GLA_DK, (hd + 1) * GLA_DK)
            vs = slice(hd * GLA_DV, (hd + 1) * GLA_DV)
            s_past = _dot_nt(q_fwd[:, ks], k_fwd[:, ks])
            s_future = _dot_nt(q_bwd[:, ks], k_bwd[:, ks])
            s_cross = _dot_nt(q_fwd[:, ks], k_st[:, ks])
            p = jnp.where(code == 0, s_past,
                          jnp.where(code == 1, s_future,
                                    jnp.where(code == 2, s_cross, 0.0))).astype(BF16)
            v = proj_s[rows, 2 * qk + hd * GLA_DV:2 * qk + (hd + 1) * GLA_DV].astype(BF16)
            st = state_s[hd]
            o = _dot(p, v) + _dot(q_t[:, ks], st.astype(BF16))
            g_col = jnp.broadcast_to(g_all[:, ks], (GLA_DK, GLA_DK)).T
            state_s[hd] = st * jnp.concatenate([g_col, g_col], axis=1) + _dot_tn(k_t[:, ks], v)
            ms = jnp.mean(o * o, axis=-1, keepdims=True)
            on = o * lax.rsqrt(ms + EPS) * og[:, vs]
            r = proj_s[rows, 2 * qk + vd + hd * GLA_DV:2 * qk + vd + (hd + 1) * GLA_DV]
            gated_s[rows, vs] = (on * (r * _sigmoid(r))).astype(BF16)
        return carry

    lax.fori_loop(0, tm // SUB, step, 0, unroll=True)
    o_ref[0] = x + _dot(gated_s[...], wo_ref[...])


def _gla_layer(h, norm_g, w_in, w_a2, b_a, out_g, w_out, *, tm=256):
    bsz, seq, d = h.shape
    qk = GLA_HEADS * GLA_DK
    vd = GLA_HEADS * GLA_DV
    nmain = 2 * qk + 2 * vd
    tm = min(tm, seq)
    w_main = w_in[:, :nmain].astype(BF16)
    w_a1 = jnp.pad(w_in[:, nmain:], ((0, 0), (0, LANES - GLA_RANK))).astype(BF16)
    w_a2p = jnp.pad(w_a2, ((0, LANES - GLA_RANK), (0, 0))).astype(BF16)
    kern = functools.partial(_gla_kernel, tm=tm)
    return pl.pallas_call(
        kern,
        grid=(bsz, seq // tm),
        in_specs=[
            pl.BlockSpec((1, tm, d), lambda b, j: (b, j, 0)),
            _const_spec((1, d)),
            _const_spec((d, nmain)),
            _const_spec((d, LANES)),
            _const_spec((LANES, qk)),
            _const_spec((1, qk)),
            _const_spec((1, vd)),
            _const_spec((vd, d)),
        ],
        out_specs=pl.BlockSpec((1, tm, d), lambda b, j: (b, j, 0)),
        out_shape=jax.ShapeDtypeStruct(h.shape, F32),
        scratch_shapes=[
            pltpu.VMEM((tm, nmain), F32),
            pltpu.VMEM((tm, qk), F32),
            pltpu.VMEM((tm, vd), BF16),
            pltpu.VMEM((GLA_HEADS, GLA_DK, GLA_DV), F32),
        ],
        compiler_params=_params(),
        name="gla_mixer",
    )(h, norm_g.reshape(1, d), w_main, w_a1, w_a2p, b_a.reshape(1, qk),
      out_g.reshape(1, vd), w_out.astype(BF16))


def _pack3(x):
    h1, h2, h3 = _split3(x)
    lane = lax.broadcasted_iota(jnp.int32, x.shape, 1)
    zero = jnp.zeros_like(h1)
    return jnp.where(lane < 32, h1, jnp.where(lane < 64, h2, jnp.where(lane < 96, h3, zero)))


def _ssd_kernel(h_ref, ng_ref, wz_ref, wx_ref, wdt_ref, cw_ref, cb_ref, dtb_ref, alog_ref,
                dsk_ref, og_ref, wo_ref, ex_ref, o_ref,
                ext_s, z_s, xbc_s, dt_s, da_s, y_s, ht_s, *, tm):
    dinner = SSD_GROUPS * SSD_HPG * SSD_HEADDIM
    gn = SSD_GROUPS * SSD_DSTATE
    gw = SSD_HPG * SSD_HEADDIM

    @pl.when(pl.program_id(1) == 0)
    def _():
        ht_s[...] = jnp.zeros_like(ht_s)
        ext_s[0:SUBLANES, :] = jnp.zeros((SUBLANES, ext_s.shape[1]), F32)

    x = h_ref[0]
    hn = _rms(x, ng_ref[...]).astype(BF16)
    z_s[...] = _dot(hn, wz_ref[...])
    ext_s[SUBLANES:SUBLANES + tm, :] = _dot(hn, wx_ref[...])
    dt = _softplus(_dot(hn, wdt_ref[...]) + dtb_ref[...])
    dt_s[...] = dt
    da_s[...] = dt * (-jnp.exp(alog_ref[...]))

    cblk = 512
    for c in range(ext_s.shape[1] // cblk):
        cs = slice(c * cblk, (c + 1) * cblk)
        acc = cb_ref[:, cs] + cw_ref[SSD_CONV - 1:SSD_CONV, cs] * ext_s[SUBLANES:SUBLANES + tm, cs]
        for kk in range(SSD_CONV - 1):
            off = SUBLANES - (SSD_CONV - 1) + kk
            acc = acc + cw_ref[kk:kk + 1, cs] * ext_s[off:off + tm, cs]
        xbc_s[:, cs] = acc * _sigmoid(acc)
    ext_s[0:SUBLANES, :] = ext_s[tm:tm + SUBLANES, :]

    ti = lax.broadcasted_iota(jnp.int32, (SUB, SUB), 0)
    si = lax.broadcasted_iota(jnp.int32, (SUB, SUB), 1)
    tri = jnp.where(si <= ti, 1.0, 0.0).astype(BF16)
    visible = (si < CHUNK) | (ti >= CHUNK)
    og = og_ref[...]
    dsk = dsk_ref[...]
    expand = ex_ref[...]

    def step(i, carry):
        rows = pl.ds(pl.multiple_of(i * SUB, SUB), SUB)
        dts = dt_s[rows, :]
        cum = _dot_exact_lhs(tri, da_s[rows, :])
        cum_end = cum[SUB - 1:SUB, :]
        dec_e = _dot(_pack3(jnp.exp(cum)), expand)
        dec_w = _dot(_pack3(dts * jnp.exp(cum_end - cum)), expand)
        cum_t = cum.T
        dt_t = dts.T
        xs = xbc_s[rows, 0:dinner]
        xs_b = xs.astype(BF16)
        xw = (xs * dec_w).astype(BF16)
        for g in range(SSD_GROUPS):
            gs = slice(g * gw, (g + 1) * gw)
            bm = xbc_s[rows, dinner + g * SSD_DSTATE:dinner + (g + 1) * SSD_DSTATE].astype(BF16)
            cm = xbc_s[rows, dinner + gn + g * SSD_DSTATE:dinner + gn + (g + 1) * SSD_DSTATE].astype(BF16)
            cb = _dot_nt(cm, bm)
            ht = ht_s[g]
            y_state = _dot(cm, ht.astype(BF16)) * dec_e[:, gs]
            ys = []
            for jj in range(SSD_HPG):
                hd = g * SSD_HPG + jj
                diff = cum[:, hd:hd + 1] - cum_t[hd:hd + 1, :]
                mix = cb * jnp.exp(-jnp.abs(diff)) * dt_t[hd:hd + 1, :]
                mix = jnp.where(visible, mix, 0.0).astype(BF16)
                ys.append(_dot(mix, xs_b[:, hd * SSD_HEADDIM:(hd + 1) * SSD_HEADDIM]))
            y = jnp.concatenate(ys, axis=1) + y_state + dsk[:, gs] * xs[:, gs]
            ht_s[g] = ht * dec_e[SUB - 1:SUB, gs] + _dot_tn(bm, xw[:, gs])
            zg = z_s[rows, gs]
            y = y * (zg * _sigmoid(zg))
            ms = jnp.mean(y * y, axis=-1, keepdims=True)
            y_s[rows, gs] = (y * lax.rsqrt(ms + EPS) * og[:, gs]).astype(BF16)
        return carry

    lax.fori_loop(0, tm // SUB, step, 0, unroll=True)
    o_ref[0] = x + _dot(y_s[...], wo_ref[...])


def _ssd_layer(h, norm_g, w_in, conv_w, conv_b, dt_bias, a_log, d_skip, out_g, w_out, *, tm=256):
    bsz, seq, d = h.shape
    heads = SSD_GROUPS * SSD_HPG
    dinner = heads * SSD_HEADDIM
    gn = SSD_GROUPS * SSD_DSTATE
    cdim = dinner + 2 * gn
    tm = min(tm, seq)

    def rep3(v):
        pad = [(0, 0)] * (v.ndim - 1) + [(0, LANES - 3 * heads)]
        return jnp.pad(jnp.concatenate([v, v, v], axis=-1), pad)

    w_z = w_in[:, :dinner].astype(BF16)
    w_x = w_in[:, dinner:dinner + cdim].astype(BF16)
    w_dt = rep3(w_in[:, dinner + cdim:]).astype(BF16)
    lane = jnp.arange(LANES)[:, None]
    col = jnp.arange(dinner)[None, :]
    expand = ((lane < 3 * heads) & ((lane % heads) == (col // SSD_HEADDIM))).astype(BF16)
    kern = functools.partial(_ssd_kernel, tm=tm)
    return pl.pallas_call(
        kern,
        grid=(bsz, seq // tm),
        in_specs=[
            pl.BlockSpec((1, tm, d), lambda b, j: (b, j, 0)),
            _const_spec((1, d)),
            _const_spec((d, dinner)),
            _const_spec((d, cdim)),
            _const_spec((d, LANES)),
            _const_spec((SSD_CONV, cdim)),
            _const_spec((1, cdim)),
            _const_spec((1, LANES)),
            _const_spec((1, LANES)),
            _const_spec((1, dinner)),
            _const_spec((1, dinner)),
            _const_spec((dinner, d)),
            _const_spec((LANES, dinner)),
        ],
        out_specs=pl.BlockSpec((1, tm, d), lambda b, j: (b, j, 0)),
        out_shape=jax.ShapeDtypeStruct(h.shape, F32),
        scratch_shapes=[
            pltpu.VMEM((tm + SUBLANES, cdim), F32),
            pltpu.VMEM((tm, dinner), F32),
            pltpu.VMEM((tm, cdim), F32),
            pltpu.VMEM((tm, LANES), F32),
            pltpu.VMEM((tm, LANES), F32),
            pltpu.VMEM((tm, dinner), BF16),
            pltpu.VMEM((SSD_GROUPS, SSD_DSTATE, SSD_HPG * SSD_HEADDIM), F32),
        ],
        compiler_params=_params(),
        name="ssd_mixer",
    )(h, norm_g.reshape(1, d), w_z, w_x, w_dt, conv_w, conv_b.reshape(1, cdim),
      rep3(dt_bias).reshape(1, LANES), rep3(a_log).reshape(1, LANES),
      jnp.repeat(d_skip, SSD_HEADDIM).reshape(1, dinner), out_g.reshape(1, dinner),
      w_out.astype(BF16), expand)


def _s5_kernel(h_ref, ng_ref, wb_ref, wc_ref, ta_ref, tp_ref, dsk_ref, wg_ref, o_ref,
               xr_s, xi_s, xs_s, y_s, car_s, *, tm):
    d = h_ref.shape[2]
    nblk = d // LANES
    sw = S5_GPB * S5_STATE
    tt = tm // SUBLANES

    @pl.when(pl.program_id(1) == 0)
    def _():
        car_s[...] = jnp.zeros_like(car_s)

    x = h_ref[0]
    u = _rms(x, ng_ref[...])
    up = jnp.swapaxes(u.reshape(SUBLANES, tt, d), 0, 1).reshape(tm, d).astype(BF16)
    row0 = lax.broadcasted_iota(jnp.int32, (SUBLANES, sw), 0) == 0

    for blk in range(nblk):
        slot = blk % 2
        bu = _dot(up[:, blk * LANES:(blk + 1) * LANES], wb_ref[blk])
        ar = ta_ref[blk, 0]
        ai = ta_ref[blk, 1]
        xr = bu[0:SUBLANES, :sw]
        xi = bu[0:SUBLANES, sw:]
        xr_s[slot, 0:SUBLANES, :] = xr
        xi_s[slot, 0:SUBLANES, :] = xi
        for t in range(1, tt):
            rows = slice(t * SUBLANES, (t + 1) * SUBLANES)
            xr, xi = bu[rows, :sw] + ar * xr - ai * xi, bu[rows, sw:] + ar * xi + ai * xr
            xr_s[slot, rows, :] = xr
            xi_s[slot, rows, :] = xi
        er = jnp.where(row0, car_s[blk, 0], pltpu.roll(xr, 1, 0))
        ei = jnp.where(row0, car_s[blk, 1], pltpu.roll(xi, 1, 0))
        for n, sh in enumerate((1, 2, 4)):
            mr, mi = ta_ref[blk, 2 + 2 * n], ta_ref[blk, 3 + 2 * n]
            rr = pltpu.roll(er, sh, 0)
            ri = pltpu.roll(ei, sh, 0)
            er, ei = er + mr * rr - mi * ri, ei + mr * ri + mi * rr
        lr = tp_ref[blk, 0, tt - 1:tt, :]
        li = tp_ref[blk, 1, tt - 1:tt, :]
        nr = xr + lr * er - li * ei
        ni = xi + lr * ei + li * er
        car_s[blk, 0] = jnp.broadcast_to(nr[SUBLANES - 1:SUBLANES, :], (SUBLANES, sw))
        car_s[blk, 1] = jnp.broadcast_to(ni[SUBLANES - 1:SUBLANES, :], (SUBLANES, sw))
        for t in range(0, tt, 2):
            outs_r, outs_i = [], []
            for t2 in (t, t + 1):
                rows = slice(t2 * SUBLANES, (t2 + 1) * SUBLANES)
                pr = tp_ref[blk, 0, t2:t2 + 1, :]
                pi = tp_ref[blk, 1, t2:t2 + 1, :]
                outs_r.append(xr_s[slot, rows, :] + pr * er - pi * ei)
                outs_i.append(xi_s[slot, rows, :] + pr * ei + pi * er)
            rows2 = slice(t * SUBLANES, (t + 2) * SUBLANES)
            xs_s[slot, rows2, :sw] = jnp.concatenate(outs_r, axis=0).astype(BF16)
            xs_s[slot, rows2, sw:] = jnp.concatenate(outs_i, axis=0).astype(BF16)
        y_s[:, blk * LANES:(blk + 1) * LANES] = _dot(xs_s[slot], wc_ref[blk])

    y = jnp.swapaxes(y_s[...].reshape(tt, SUBLANES, d), 0, 1).reshape(tm, d) + dsk_ref[...] * u
    c0 = math.sqrt(2.0 / math.pi)
    y = y * (0.5 * (1.0 + jnp.tanh(c0 * (y + 0.044715 * (y * y * y)))))
    vg = _dot(y.astype(BF16), wg_ref[...])
    o_ref[0] = x + vg[:, :d] * _sigmoid(vg[:, d:])


def _s5_tables(log_dt, a_re, a_im, b_re, b_im, c_re, c_im, tt):
    groups, nstate = a_re.shape
    nblk = groups // S5_GPB
    sw = S5_GPB * S5_STATE
    step = jnp.exp(log_dt.astype(F32))[:, None]
    a_re = a_re.astype(F32)
    a_im = a_im.astype(F32)
    mag = jnp.exp(step * a_re)
    abar_re = mag * jnp.cos(step * a_im)
    abar_im = mag * jnp.sin(step * a_im)
    den = a_re * a_re + a_im * a_im
    num_re = abar_re - 1.0
    num_im = abar_im
    f_re = (num_re * a_re + num_im * a_im) / den
    f_im = (num_im * a_re - num_re * a_im) / den
    b_re = b_re.astype(F32)
    b_im = b_im.astype(F32)
    bb_re = f_re[..., None] * b_re - f_im[..., None] * b_im
    bb_im = f_re[..., None] * b_im + f_im[..., None] * b_re
    eye = jnp.eye(S5_GPB, dtype=F32)

    def in_mat(bb):
        t = bb.reshape(nblk, S5_GPB, nstate, S5_GROUP)
        m = jnp.einsum('bgpc,gh->bgchp', t, eye)
        return m.reshape(nblk, LANES, sw)

    def out_mat(cc):
        t = cc.reshape(nblk, S5_GPB, S5_GROUP, nstate)
        m = jnp.einsum('bgcp,gh->bgphc', t, eye)
        return m.reshape(nblk, sw, LANES)

    w_b = jnp.concatenate([in_mat(bb_re), in_mat(bb_im)], axis=2).astype(BF16)
    w_c = jnp.concatenate([out_mat(c_re.astype(F32)), -out_mat(c_im.astype(F32))], axis=1).astype(BF16)

    def power(k):
        m = jnp.exp(k * step * a_re)
        return m * jnp.cos(k * step * a_im), m * jnp.sin(k * step * a_im)

    def lay(t):
        return t.reshape(t.shape[0], nblk, sw).transpose(1, 0, 2)

    row = jnp.arange(SUBLANES, dtype=F32)[:, None, None]
    one = jnp.ones((SUBLANES, 1, 1), F32)
    pr, pi = power(one)
    tabs = [lay(pr), lay(pi)]
    for sh in (1, 2, 4):
        pr, pi = power(one * float(sh * tt))
        keep = (row >= sh).astype(F32)
        tabs += [lay(keep * pr), lay(keep * pi)]
    tab_a = jnp.stack(tabs, axis=1)
    tau = (jnp.arange(tt, dtype=F32) + 1.0)[:, None, None]
    pr, pi = power(tau)
    tab_p = jnp.stack([lay(pr), lay(pi)], axis=1)
    return w_b, w_c, tab_a, tab_p


def _s5_layer(h, norm_g, log_dt, a_re, a_im, b_re, b_im, c_re, c_im, d_skip, w_glu, *, tm=512):
    bsz, seq, d = h.shape
    nblk = d // LANES
    sw = S5_GPB * S5_STATE
    tm = min(tm, seq)
    tt = tm // SUBLANES
    w_b, w_c, tab_a, tab_p = _s5_tables(log_dt, a_re, a_im, b_re, b_im, c_re, c_im, tt)
    kern = functools.partial(_s5_kernel, tm=tm)
    return pl.pallas_call(
        kern,
        grid=(bsz, seq // tm),
        in_specs=[
            pl.BlockSpec((1, tm, d), lambda b, j: (b, j, 0)),
            _const_spec((1, d)),
            _const_spec((nblk, LANES, 2 * sw)),
            _const_spec((nblk, 2 * sw, LANES)),
            _const_spec((nblk, 8, SUBLANES, sw)),
            _const_spec((nblk, 2, tt, sw)),
            _const_spec((1, d)),
            _const_spec((d, 2 * d)),
        ],
        out_specs=pl.BlockSpec((1, tm, d), lambda b, j: (b, j, 0)),
        out_shape=jax.ShapeDtypeStruct(h.shape, F32),
        scratch_shapes=[
            pltpu.VMEM((2, tm, sw), F32),
            pltpu.VMEM((2, tm, sw), F32),
            pltpu.VMEM((2, tm, 2 * sw), BF16),
            pltpu.VMEM((tm, d), F32),
            pltpu.VMEM((nblk, 2, SUBLANES, sw), F32),
        ],
        compiler_params=_params(),
        name="s5_mixer",
    )(h, norm_g.reshape(1, d), w_b, w_c, tab_a, tab_p, d_skip.astype(F32).reshape(1, d),
      w_glu.astype(BF16))


def kernel(x, norm_mix_g, norm_ffn_g, gla_w_in, gla_w_a2, gla_b_a, gla_norm_g, gla_w_out, ssd_w_in, ssd_conv_w, ssd_conv_b, ssd_dt_bias, ssd_a_log, ssd_d, ssd_norm_g, ssd_w_out, s5_log_dt, s5_a_re, s5_a_im, s5_b_re, s5_b_im, s5_c_re, s5_c_im, s5_d, s5_w_glu, ffn_w_gu, ffn_w_down, final_norm_g):
    depth = norm_mix_g.shape[0]
    n_mixers = 3
    h = x
    for i in range(depth):
        mixer, j = i % n_mixers, i // n_mixers
        if mixer == 0:
            h = _gla_layer(h, norm_mix_g[i], gla_w_in[j], gla_w_a2[j], gla_b_a[j],
                           gla_norm_g[j], gla_w_out[j])
        elif mixer == 1:
            h = _ssd_layer(h, norm_mix_g[i], ssd_w_in[j], ssd_conv_w[j], ssd_conv_b[j],
                           ssd_dt_bias[j], ssd_a_log[j], ssd_d[j], ssd_norm_g[j], ssd_w_out[j])
        else:
            h = _s5_layer(h, norm_mix_g[i], s5_log_dt[j], s5_a_re[j], s5_a_im[j], s5_b_re[j],
                          s5_b_im[j], s5_c_re[j], s5_c_im[j], s5_d[j], s5_w_glu[j])
        h = _ffn_layer(h, norm_ffn_g[i], ffn_w_gu[i], ffn_w_down[i],
                       final_norm_g if i == depth - 1 else None)
    return h
```

```python
import functools
import math

import jax
import jax.numpy as jnp
from jax import lax
from jax.experimental import pallas as pl
from jax.experimental.pallas import tpu as pltpu

F32 = jnp.float32
BF16 = jnp.bfloat16

EPS = 1e-6
CHUNK = 64
SUB = 2 * CHUNK
LANES = 128
SUBLANES = 8

GLA_HEADS = 4
GLA_DK = 128
GLA_DV = 256
GLA_RANK = 16
GLA_TAU = 16.0
SSD_HEADDIM = 64
SSD_GROUPS = 8
SSD_HPG = 4
SSD_DSTATE = 128
SSD_CONV = 4
S5_GROUP = 16
S5_STATE = 64
S5_GPB = LANES // S5_GROUP

VMEM_LIMIT = 56 * 1024 * 1024


def _dot(a, b):
    return jnp.dot(a, b, preferred_element_type=F32)


def _dot_nt(a, b):
    return lax.dot_general(a, b, (((1,), (1,)), ((), ())), preferred_element_type=F32)


def _dot_tn(a, b):
    return lax.dot_general(a, b, (((0,), (0,)), ((), ())), preferred_element_type=F32)


def _split3(x):
    h1 = x.astype(BF16)
    r1 = x - h1.astype(F32)
    h2 = r1.astype(BF16)
    r2 = r1 - h2.astype(F32)
    return h1, h2, r2.astype(BF16)


def _dot_exact_lhs(t, x):
    h1, h2, h3 = _split3(x)
    return _dot(t, h1) + _dot(t, h2) + _dot(t, h3)


def _rms(x, g):
    ms = jnp.mean(x * x, axis=-1, keepdims=True)
    return x * lax.rsqrt(ms + EPS) * g


def _sigmoid(x):
    return 1.0 / (1.0 + jnp.exp(-x))


def _softplus(x):
    return jnp.maximum(x, 0.0) + jnp.log1p(jnp.exp(-jnp.abs(x)))


def _const_spec(shape):
    nd = len(shape)
    return pl.BlockSpec(shape, lambda *_: (0,) * nd, pipeline_mode=pl.Buffered(1))


def _params(grid_rank=2):
    return pltpu.CompilerParams(
        dimension_semantics=("arbitrary",) * grid_rank,
        vmem_limit_bytes=VMEM_LIMIT)


def _ffn_kernel(h_ref, ng_ref, wgu_ref, wd_ref, fg_ref, o_ref, *, hidden, final):
    x = h_ref[0]
    xn = _rms(x, ng_ref[...]).astype(BF16)
    gu = _dot(xn, wgu_ref[...])
    g = gu[:, :hidden]
    u = gu[:, hidden:]
    a = (g * _sigmoid(g) * u).astype(BF16)
    out = x + _dot(a, wd_ref[...])
    if final:
        out = _rms(out, fg_ref[...])
    o_ref[0] = out


def _ffn_layer(h, norm_g, w_gu, w_down, final_g, *, tm=512):
    bsz, seq, d = h.shape
    hidden = w_down.shape[0]
    tm = min(tm, seq)
    final = final_g is not None
    fg = (final_g if final else norm_g).reshape(1, d)
    kern = functools.partial(_ffn_kernel, hidden=hidden, final=final)
    return pl.pallas_call(
        kern,
        grid=(bsz, seq // tm),
        in_specs=[
            pl.BlockSpec((1, tm, d), lambda b, j: (b, j, 0)),
            _const_spec((1, d)),
            _const_spec((d, 2 * hidden)),
            _const_spec((hidden, d)),
            _const_spec((1, d)),
        ],
        out_specs=pl.BlockSpec((1, tm, d), lambda b, j: (b, j, 0)),
        out_shape=jax.ShapeDtypeStruct(h.shape, F32),
        compiler_params=_params(),
        name="swiglu_ffn",
    )(h, norm_g.reshape(1, d), w_gu.astype(BF16), w_down.astype(BF16), fg)


def _pipe_specs(bsz, seq, tm, d):
    nt = seq // tm
    n_tiles = bsz * nt

    def next_a(i):
        t = jnp.minimum(2 * i + 2, n_tiles - 1)
        return (t // nt, t % nt, 0)

    def pair(i):
        return (i // (nt // 2), i % (nt // 2), 0)

    in_next = pl.BlockSpec((1, tm, d), next_a)
    in_pair = pl.BlockSpec((1, 2 * tm, d), pair)
    out_pair = pl.BlockSpec((1, 2 * tm, d), pair)
    return nt, n_tiles // 2, in_next, in_pair, out_pair


def _interleave(*stages):
    live = [iter(st) for st in stages]
    while live:
        for st in list(live):
            try:
                next(st)
            except StopIteration:
                live.remove(st)


def _gla_stage1(x, ng_ref, w_ref, wa1_ref, wa2_ref, ba_ref, proj_o, la_o):
    nmain = proj_o.shape[1]
    hn = _rms(x, ng_ref[...]).astype(BF16)
    a_low = _dot(hn, wa1_ref[...]).astype(BF16)
    a = _dot(a_low, wa2_ref[...]) + ba_ref[...]
    la_o[...] = (jnp.minimum(a, 0.0) - jnp.log1p(jnp.exp(-jnp.abs(a)))) * (1.0 / GLA_TAU)
    yield
    cblk = 512
    for c in range(nmain // cblk):
        proj_o[:, c * cblk:(c + 1) * cblk] = _dot(hn, w_ref[:, c * cblk:(c + 1) * cblk])
        yield


def _gla_stage2(x, o_ref, out_rows, first_tile, proj_i, la_i, gated_s, state_s, og_ref, wo_ref, *, tm):
    qk = GLA_HEADS * GLA_DK
    vd = GLA_HEADS * GLA_DV
    ti = lax.broadcasted_iota(jnp.int32, (SUB, SUB), 0)
    si = lax.broadcasted_iota(jnp.int32, (SUB, SUB), 1)
    same = (ti < CHUNK) == (si < CHUNK)
    tri = jnp.where(same & (si <= ti), 1.0, 0.0).astype(BF16)
    code = jnp.where(same, jnp.where(si <= ti, 0, 1), jnp.where(si < ti, 2, 3))
    first = lax.broadcasted_iota(jnp.int32, (SUB, 1), 0) < CHUNK
    og = og_ref[...]

    for sub in range(tm // SUB):
        rows = slice(sub * SUB, (sub + 1) * SUB)
        lc = _dot_exact_lhs(tri, la_i[rows, :])
        lend0 = lc[CHUNK - 1:CHUNK, :]
        lend1 = lc[SUB - 1:SUB, :]
        lend = jnp.where(first, lend0, lend1)
        e_pos = jnp.exp(lc)
        e_neg = jnp.exp(-lc)
        e_st = jnp.exp(lend - lc)
        g0 = jnp.exp(lend0)
        g1 = jnp.exp(lend1)
        q = proj_i[rows, 0:qk] * (GLA_DK ** -0.5)
        k = proj_i[rows, qk:2 * qk]
        q_fwd = q * e_pos
        q_bwd = (q * e_neg).astype(BF16)
        k_fwd = (k * e_neg).astype(BF16)
        k_bwd = (k * e_pos).astype(BF16)
        k_st = k * e_st
        q_t = (q_fwd * jnp.where(first, 1.0, g0)).astype(BF16)
        k_t = (k_st * jnp.where(first, g1, 1.0)).astype(BF16)
        q_fwd = q_fwd.astype(BF16)
        k_st = k_st.astype(BF16)
        g_all = g0 * g1
        yield
        for hd in range(GLA_HEADS):
            ks = slice(hd * GLA_DK, (hd + 1) * GLA_DK)
            vs = slice(hd * GLA_DV, (hd + 1) * GLA_DV)
            s_past = _dot_nt(q_fwd[:, ks], k_fwd[:, ks])
            s_future = _dot_nt(q_bwd[:, ks], k_bwd[:, ks])
            s_cross = _dot_nt(q_fwd[:, ks], k_st[:, ks])
            p = jnp.where(code == 0, s_past,
                          jnp.where(code == 1, s_future,
                                    jnp.where(code == 2, s_cross, 0.0))).astype(BF16)
            v = proj_i[rows, 2 * qk + hd * GLA_DV:2 * qk + (hd + 1) * GLA_DV].astype(BF16)
            st = state_s[hd]
            if sub == 0:
                st = jnp.where(first_tile, 0.0, st)
            o = _dot(p, v) + _dot(q_t[:, ks], st.astype(BF16))
            g_col = jnp.broadcast_to(g_all[:, ks], (GLA_DK, GLA_DK)).T
            state_s[hd] = st * jnp.concatenate([g_col, g_col], axis=1) + _dot_tn(k_t[:, ks], v)
            ms = jnp.mean(o * o, axis=-1, keepdims=True)
            on = o * lax.rsqrt(ms + EPS) * og[:, vs]
            r = proj_i[rows, 2 * qk + vd + hd * GLA_DV:2 * qk + vd + (hd + 1) * GLA_DV]
            gated_s[rows, vs] = (on * (r * _sigmoid(r))).astype(BF16)
            yield
    o_ref[0, out_rows, :] = x + _dot(gated_s[...], wo_ref[...])


def _gla_kernel(hn_ref, hp_ref, ng_ref, w_ref, wa1_ref, wa2_ref, ba_ref, og_ref, wo_ref, o_ref,
                proj_a, proj_b, la_a, la_b, gat_a, gat_b, state_s, *, tm, nt):
    i = pl.program_id(0)
    stage1 = functools.partial(_gla_stage1, ng_ref=ng_ref, w_ref=w_ref, wa1_ref=wa1_ref,
                               wa2_ref=wa2_ref, ba_ref=ba_ref)
    stage2 = functools.partial(_gla_stage2, state_s=state_s, og_ref=og_ref, wo_ref=wo_ref, tm=tm)
    rows_a = slice(0, tm)
    rows_b = slice(tm, 2 * tm)

    @pl.when(i == 0)
    def _():
        _interleave(stage1(hp_ref[0, rows_a, :], proj_o=proj_a, la_o=la_a))

    first_a = (2 * i) % nt == 0
    _interleave(stage2(hp_ref[0, rows_a, :], o_ref, rows_a, first_a, proj_a, la_a, gat_a),
                stage1(hp_ref[0, rows_b, :], proj_o=proj_b, la_o=la_b))
    _interleave(stage2(hp_ref[0, rows_b, :], o_ref, rows_b, False, proj_b, la_b, gat_b),
                stage1(hn_ref[0], proj_o=proj_a, la_o=la_a))


def _gla_layer(h, norm_g, w_in, w_a2, b_a, out_g, w_out, *, tm=256):
    bsz, seq, d = h.shape
    qk = GLA_HEADS * GLA_DK
    vd = GLA_HEADS * GLA_DV
    nmain = 2 * qk + 2 * vd
    tm = min(tm, seq // 2)
    nt, steps, in_next, in_pair, out_pair = _pipe_specs(bsz, seq, tm, d)
    w_a1 = jnp.pad(w_in[:, nmain:], ((0, 0), (0, LANES - GLA_RANK))).astype(BF16)
    w_a2p = jnp.pad(w_a2, ((0, LANES - GLA_RANK), (0, 0))).astype(BF16)
    kern = functools.partial(_gla_kernel, tm=tm, nt=nt)
    return pl.pallas_call(
        kern,
        grid=(steps,),
        in_specs=[
            in_next,
            in_pair,
            _const_spec((1, d)),
            _const_spec(w_in.shape),
            _const_spec((d, LANES)),
            _const_spec((LANES, qk)),
            _const_spec((1, qk)),
            _const_spec((1, vd)),
            _const_spec((vd, d)),
        ],
        out_specs=out_pair,
        out_shape=jax.ShapeDtypeStruct(h.shape, F32),
        scratch_shapes=[
            pltpu.VMEM((tm, nmain), F32), pltpu.VMEM((tm, nmain), F32),
            pltpu.VMEM((tm, qk), F32), pltpu.VMEM((tm, qk), F32),
            pltpu.VMEM((tm, vd), BF16), pltpu.VMEM((tm, vd), BF16),
            pltpu.VMEM((GLA_HEADS, GLA_DK, GLA_DV), F32),
        ],
        compiler_params=_params(1),
        name="gla_mixer",
    )(h, h, norm_g.reshape(1, d), w_in.astype(BF16), w_a1, w_a2p, b_a.reshape(1, qk),
      out_g.reshape(1, vd), w_out.astype(BF16))


def _pack3(x):
    h1, h2, h3 = _split3(x)
    lane = lax.broadcasted_iota(jnp.int32, x.shape, 1)
    zero = jnp.zeros_like(h1)
    return jnp.where(lane < 32, h1, jnp.where(lane < 64, h2, jnp.where(lane < 96, h3, zero)))


def _ssd_stage1(x, first_tile, ng_ref, w_ref, wdt_ref, cw_ref, cb_ref, dtb_ref, alog_ref,
                ext_s, tail_s, xbc_o, gz_o, dt_o, da_o, *, tm):
    dinner = gz_o.shape[1]
    cdim = xbc_o.shape[1]
    hn = _rms(x, ng_ref[...]).astype(BF16)
    dt = _softplus(_dot(hn, wdt_ref[...]) + dtb_ref[...])
    dt_o[...] = dt
    da_o[...] = dt * (-jnp.exp(alog_ref[...]))
    yield
    cblk = 512
    for c in range(dinner // cblk):
        cs = slice(c * cblk, (c + 1) * cblk)
        z = _dot(hn, w_ref[:, cs])
        gz_o[:, cs] = z * _sigmoid(z)
        yield
    for c in range(cdim // cblk):
        cs = slice(c * cblk, (c + 1) * cblk)
        ext_s[SUBLANES:SUBLANES + tm, cs] = _dot(hn, w_ref[:, dinner + c * cblk:dinner + (c + 1) * cblk])
        ext_s[0:SUBLANES, cs] = jnp.where(first_tile, 0.0, tail_s[:, cs])
        acc = cb_ref[:, cs] + cw_ref[SSD_CONV - 1:SSD_CONV, cs] * ext_s[SUBLANES:SUBLANES + tm, cs]
        for kk in range(SSD_CONV - 1):
            off = SUBLANES - (SSD_CONV - 1) + kk
            acc = acc + cw_ref[kk:kk + 1, cs] * ext_s[off:off + tm, cs]
        xbc_o[:, cs] = acc * _sigmoid(acc)
        tail_s[:, cs] = ext_s[tm:tm + SUBLANES, cs]
        yield


def _ssd_stage2(x, o_ref, out_rows, first_tile, xbc_i, gz_i, dt_i, da_i, y_s, ht_s, dsk_ref, og_ref,
                wo_ref, ex_ref, *, tm):
    dinner = gz_i.shape[1]
    gn = SSD_GROUPS * SSD_DSTATE
    gw = SSD_HPG * SSD_HEADDIM
    ti = lax.broadcasted_iota(jnp.int32, (SUB, SUB), 0)
    si = lax.broadcasted_iota(jnp.int32, (SUB, SUB), 1)
    tri = jnp.where(si <= ti, 1.0, 0.0).astype(BF16)
    visible = (si < CHUNK) | (ti >= CHUNK)
    head_of_lane = lax.broadcasted_iota(jnp.int32, (SUB, gw), 1) // SSD_HEADDIM
    og = og_ref[...]
    dsk = dsk_ref[...]
    expand = ex_ref[...]
    neg_log2e = -1.4426950408889634

    for sub in range(tm // SUB):
        rows = slice(sub * SUB, (sub + 1) * SUB)
        dts = dt_i[rows, :]
        cum = _dot_exact_lhs(tri, da_i[rows, :])
        cum_end = cum[SUB - 1:SUB, :]
        dec_e = _dot(_pack3(jnp.exp(cum)), expand)
        dec_w = _dot(_pack3(dts * jnp.exp(cum_end - cum)), expand)
        cum_t = cum.T
        dt_t = dts.T
        xs = xbc_i[rows, 0:dinner]
        xs_b = xs.astype(BF16)
        xw = (xs * dec_w).astype(BF16)
        yield
        for g in range(SSD_GROUPS):
            gs = slice(g * gw, (g + 1) * gw)
            bm = xbc_i[rows, dinner + g * SSD_DSTATE:dinner + (g + 1) * SSD_DSTATE].astype(BF16)
            cm = xbc_i[rows, dinner + gn + g * SSD_DSTATE:dinner + gn + (g + 1) * SSD_DSTATE].astype(BF16)
            cb = jnp.where(visible, _dot_nt(cm, bm), 0.0)
            ht = ht_s[g]
            if sub == 0:
                ht = jnp.where(first_tile, 0.0, ht)
            y_state = _dot(cm, ht.astype(BF16)) * dec_e[:, gs]
            mixes = []
            diag = []
            for jj in range(SSD_HPG):
                hd = g * SSD_HPG + jj
                diff = cum[:, hd:hd + 1] - cum_t[hd:hd + 1, :]
                mix = cb * jnp.exp2(jnp.abs(diff) * neg_log2e) * dt_t[hd:hd + 1, :]
                mixes.append(mix.astype(BF16))
                diag.append(jnp.where(head_of_lane == jj, xs_b[:, gs], jnp.zeros_like(xs_b[:, gs])))
            y = _dot(jnp.concatenate(mixes, axis=1), jnp.concatenate(diag, axis=0))
            y = y + y_state + dsk[:, gs] * xs[:, gs]
            ht_s[g] = ht * dec_e[SUB - 1:SUB, gs] + _dot_tn(bm, xw[:, gs])
            y = y * gz_i[rows, gs]
            ms = jnp.mean(y * y, axis=-1, keepdims=True)
            y_s[rows, gs] = (y * lax.rsqrt(ms + EPS) * og[:, gs]).astype(BF16)
            yield
    o_ref[0, out_rows, :] = x + _dot(y_s[...], wo_ref[...])


def _ssd_kernel(hn_ref, hp_ref, ng_ref, w_ref, wdt_ref, cw_ref, cb_ref, dtb_ref, alog_ref,
                dsk_ref, og_ref, wo_ref, ex_ref, o_ref,
                ext_s, tail_s, xbc_a, xbc_b, gz_a, gz_b, dt_a, dt_b, da_a, da_b, y_a, y_b, ht_s,
                *, tm, nt):
    i = pl.program_id(0)
    stage1 = functools.partial(_ssd_stage1, ng_ref=ng_ref, w_ref=w_ref, wdt_ref=wdt_ref,
                               cw_ref=cw_ref, cb_ref=cb_ref, dtb_ref=dtb_ref, alog_ref=alog_ref,
                               ext_s=ext_s, tail_s=tail_s, tm=tm)
    stage2 = functools.partial(_ssd_stage2, ht_s=ht_s, dsk_ref=dsk_ref, og_ref=og_ref,
                               wo_ref=wo_ref, ex_ref=ex_ref, tm=tm)
    rows_a = slice(0, tm)
    rows_b = slice(tm, 2 * tm)

    @pl.when(i == 0)
    def _():
        _interleave(stage1(hp_ref[0, rows_a, :], True, xbc_o=xbc_a, gz_o=gz_a, dt_o=dt_a, da_o=da_a))

    first_a = (2 * i) % nt == 0
    first_next = (2 * i + 2) % nt == 0
    _interleave(stage2(hp_ref[0, rows_a, :], o_ref, rows_a, first_a, xbc_a, gz_a, dt_a, da_a, y_a),
                stage1(hp_ref[0, rows_b, :], False, xbc_o=xbc_b, gz_o=gz_b, dt_o=dt_b, da_o=da_b))
    _interleave(stage2(hp_ref[0, rows_b, :], o_ref, rows_b, False, xbc_b, gz_b, dt_b, da_b, y_b),
                stage1(hn_ref[0], first_next, xbc_o=xbc_a, gz_o=gz_a, dt_o=dt_a, da_o=da_a))


def _ssd_layer(h, norm_g, w_in, conv_w, conv_b, dt_bias, a_log, d_skip, out_g, w_out, *, tm=256):
    bsz, seq, d = h.shape
    heads = SSD_GROUPS * SSD_HPG
    dinner = heads * SSD_HEADDIM
    gn = SSD_GROUPS * SSD_DSTATE
    cdim = dinner + 2 * gn
    tm = min(tm, seq // 2)
    nt, steps, in_next, in_pair, out_pair = _pipe_specs(bsz, seq, tm, d)

    def rep3(v):
        pad = [(0, 0)] * (v.ndim - 1) + [(0, LANES - 3 * heads)]
        return jnp.pad(jnp.concatenate([v, v, v], axis=-1), pad)

    w_dt = rep3(w_in[:, dinner + cdim:]).astype(BF16)
    lane = jnp.arange(LANES)[:, None]
    col = jnp.arange(dinner)[None, :]
    expand = ((lane < 3 * heads) & ((lane % heads) == (col // SSD_HEADDIM))).astype(BF16)
    kern = functools.partial(_ssd_kernel, tm=tm, nt=nt)
    slot = lambda shape, dt: [pltpu.VMEM(shape, dt), pltpu.VMEM(shape, dt)]
    return pl.pallas_call(
        kern,
        grid=(steps,),
        in_specs=[
            in_next,
            in_pair,
            _const_spec((1, d)),
            _const_spec(w_in.shape),
            _const_spec((d, LANES)),
            _const_spec((SSD_CONV, cdim)),
            _const_spec((1, cdim)),
            _const_spec((1, LANES)),
            _const_spec((1, LANES)),
            _const_spec((1, dinner)),
            _const_spec((1, dinner)),
            _const_spec((dinner, d)),
            _const_spec((LANES, dinner)),
        ],
        out_specs=out_pair,
        out_shape=jax.ShapeDtypeStruct(h.shape, F32),
        scratch_shapes=[
            pltpu.VMEM((tm + SUBLANES, cdim), F32),
            pltpu.VMEM((SUBLANES, cdim), F32),
            *slot((tm, cdim), F32),
            *slot((tm, dinner), F32),
            *slot((tm, LANES), F32),
            *slot((tm, LANES), F32),
            *slot((tm, dinner), BF16),
            pltpu.VMEM((SSD_GROUPS, SSD_DSTATE, SSD_HPG * SSD_HEADDIM), F32),
        ],
        compiler_params=_params(1),
        name="ssd_mixer",
    )(h, h, norm_g.reshape(1, d), w_in.astype(BF16), w_dt, conv_w, conv_b.reshape(1, cdim),
      rep3(dt_bias).reshape(1, LANES), rep3(a_log).reshape(1, LANES),
      jnp.repeat(d_skip, SSD_HEADDIM).reshape(1, dinner), out_g.reshape(1, dinner),
      w_out.astype(BF16), expand)


def _s5_kernel(h_ref, ng_ref, wb_ref, wc_ref, ta_ref, tp_ref, dsk_ref, wg_ref, o_ref,
               xr_s, xi_s, xs_s, y_s, car_s, *, tm):
    d = h_ref.shape[2]
    nblk = d // LANES
    sw = S5_GPB * S5_STATE
    tt = tm // SUBLANES

    @pl.when(pl.program_id(1) == 0)
    def _():
        car_s[...] = jnp.zeros_like(car_s)

    x = h_ref[0]
    u = _rms(x, ng_ref[...])
    up = jnp.swapaxes(u.reshape(SUBLANES, tt, d), 0, 1).reshape(tm, d).astype(BF16)
    row0 = lax.broadcasted_iota(jnp.int32, (SUBLANES, sw), 0) == 0

    for blk in range(nblk):
        slot = blk % 2
        bu = _dot(up[:, blk * LANES:(blk + 1) * LANES], wb_ref[blk])
        ar = ta_ref[blk, 0]
        ai = ta_ref[blk, 1]
        xr = bu[0:SUBLANES, :sw]
        xi = bu[0:SUBLANES, sw:]
        xr_s[slot, 0:SUBLANES, :] = xr
        xi_s[slot, 0:SUBLANES, :] = xi
        for t in range(1, tt):
            rows = slice(t * SUBLANES, (t + 1) * SUBLANES)
            xr, xi = bu[rows, :sw] + ar * xr - ai * xi, bu[rows, sw:] + ar * xi + ai * xr
            xr_s[slot, rows, :] = xr
            xi_s[slot, rows, :] = xi
        er = jnp.where(row0, car_s[blk, 0], pltpu.roll(xr, 1, 0))
        ei = jnp.where(row0, car_s[blk, 1], pltpu.roll(xi, 1, 0))
        for n, sh in enumerate((1, 2, 4)):
            mr, mi = ta_ref[blk, 2 + 2 * n], ta_ref[blk, 3 + 2 * n]
            rr = pltpu.roll(er, sh, 0)
            ri = pltpu.roll(ei, sh, 0)
            er, ei = er + mr * rr - mi * ri, ei + mr * ri + mi * rr
        lr = tp_ref[blk, 0, tt - 1:tt, :]
        li = tp_ref[blk, 1, tt - 1:tt, :]
        nr = xr + lr * er - li * ei
        ni = xi + lr * ei + li * er
        car_s[blk, 0] = jnp.broadcast_to(nr[SUBLANES - 1:SUBLANES, :], (SUBLANES, sw))
        car_s[blk, 1] = jnp.broadcast_to(ni[SUBLANES - 1:SUBLANES, :], (SUBLANES, sw))
        for t in range(0, tt, 2):
            outs_r, outs_i = [], []
            for t2 in (t, t + 1):
                rows = slice(t2 * SUBLANES, (t2 + 1) * SUBLANES)
                pr = tp_ref[blk, 0, t2:t2 + 1, :]
                pi = tp_ref[blk, 1, t2:t2 + 1, :]
                outs_r.append(xr_s[slot, rows, :] + pr * er - pi * ei)
                outs_i.append(xi_s[slot, rows, :] + pr * ei + pi * er)
            rows2 = slice(t * SUBLANES, (t + 2) * SUBLANES)
            xs_s[slot, rows2, :sw] = jnp.concatenate(outs_r, axis=0).astype(BF16)
            xs_s[slot, rows2, sw:] = jnp.concatenate(outs_i, axis=0).astype(BF16)
        y_s[:, blk * LANES:(blk + 1) * LANES] = _dot(xs_s[slot], wc_ref[blk])

    y = jnp.swapaxes(y_s[...].reshape(tt, SUBLANES, d), 0, 1).reshape(tm, d) + dsk_ref[...] * u
    c0 = math.sqrt(2.0 / math.pi)
    y = y * (0.5 * (1.0 + jnp.tanh(c0 * (y + 0.044715 * (y * y * y)))))
    vg = _dot(y.astype(BF16), wg_ref[...])
    o_ref[0] = x + vg[:, :d] * _sigmoid(vg[:, d:])


def _s5_tables(log_dt, a_re, a_im, b_re, b_im, c_re, c_im, tt):
    groups, nstate = a_re.shape
    nblk = groups // S5_GPB
    sw = S5_GPB * S5_STATE
    step = jnp.exp(log_dt.astype(F32))[:, None]
    a_re = a_re.astype(F32)
    a_im = a_im.astype(F32)
    mag = jnp.exp(step * a_re)
    abar_re = mag * jnp.cos(step * a_im)
    abar_im = mag * jnp.sin(step * a_im)
    den = a_re * a_re + a_im * a_im
    num_re = abar_re - 1.0
    num_im = abar_im
    f_re = (num_re * a_re + num_im * a_im) / den
    f_im = (num_im * a_re - num_re * a_im) / den
    b_re = b_re.astype(F32)
    b_im = b_im.astype(F32)
    bb_re = f_re[..., None] * b_re - f_im[..., None] * b_im
    bb_im = f_re[..., None] * b_im + f_im[..., None] * b_re
    eye = jnp.eye(S5_GPB, dtype=F32)

    def in_mat(bb):
        t = bb.reshape(nblk, S5_GPB, nstate, S5_GROUP)
        m = jnp.einsum('bgpc,gh->bgchp', t, eye)
        return m.reshape(nblk, LANES, sw)

    def out_mat(cc):
        t = cc.reshape(nblk, S5_GPB, S5_GROUP, nstate)
        m = jnp.einsum('bgcp,gh->bgphc', t, eye)
        return m.reshape(nblk, sw, LANES)

    w_b = jnp.concatenate([in_mat(bb_re), in_mat(bb_im)], axis=2).astype(BF16)
    w_c = jnp.concatenate([out_mat(c_re.astype(F32)), -out_mat(c_im.astype(F32))], axis=1).astype(BF16)

    def power(k):
        m = jnp.exp(k * step * a_re)
        return m * jnp.cos(k * step * a_im), m * jnp.sin(k * step * a_im)

    def lay(t):
        return t.reshape(t.shape[0], nblk, sw).transpose(1, 0, 2)

    row = jnp.arange(SUBLANES, dtype=F32)[:, None, None]
    one = jnp.ones((SUBLANES, 1, 1), F32)
    pr, pi = power(one)
    tabs = [lay(pr), lay(pi)]
    for sh in (1, 2, 4):
        pr, pi = power(one * float(sh * tt))
        keep = (row >= sh).astype(F32)
        tabs += [lay(keep * pr), lay(keep * pi)]
    tab_a = jnp.stack(tabs, axis=1)
    tau = (jnp.arange(tt, dtype=F32) + 1.0)[:, None, None]
    pr, pi = power(tau)
    tab_p = jnp.stack([lay(pr), lay(pi)], axis=1)
    return w_b, w_c, tab_a, tab_p


def _s5_layer(h, norm_g, log_dt, a_re, a_im, b_re, b_im, c_re, c_im, d_skip, w_glu, *, tm=512):
    bsz, seq, d = h.shape
    nblk = d // LANES
    sw = S5_GPB * S5_STATE
    tm = min(tm, seq)
    tt = tm // SUBLANES
    w_b, w_c, tab_a, tab_p = _s5_tables(log_dt, a_re, a_im, b_re, b_im, c_re, c_im, tt)
    kern = functools.partial(_s5_kernel, tm=tm)
    return pl.pallas_call(
        kern,
        grid=(bsz, seq // tm),
        in_specs=[
            pl.BlockSpec((1, tm, d), lambda b, j: (b, j, 0)),
            _const_spec((1, d)),
            _const_spec((nblk, LANES, 2 * sw)),
            _const_spec((nblk, 2 * sw, LANES)),
            _const_spec((nblk, 8, SUBLANES, sw)),
            _const_spec((nblk, 2, tt, sw)),
            _const_spec((1, d)),
            _const_spec((d, 2 * d)),
        ],
        out_specs=pl.BlockSpec((1, tm, d), lambda b, j: (b, j, 0)),
        out_shape=jax.ShapeDtypeStruct(h.shape, F32),
        scratch_shapes=[
            pltpu.VMEM((2, tm, sw), F32),
            pltpu.VMEM((2, tm, sw), F32),
            pltpu.VMEM((2, tm, 2 * sw), BF16),
            pltpu.VMEM((tm, d), F32),
            pltpu.VMEM((nblk, 2, SUBLANES, sw), F32),
        ],
        compiler_params=_params(),
        name="s5_mixer",
    )(h, norm_g.reshape(1, d), w_b, w_c, tab_a, tab_p, d_skip.astype(F32).reshape(1, d),
      w_glu.astype(BF16))


def kernel(x, norm_mix_g, norm_ffn_g, gla_w_in, gla_w_a2, gla_b_a, gla_norm_g, gla_w_out, ssd_w_in, ssd_conv_w, ssd_conv_b, ssd_dt_bias, ssd_a_log, ssd_d, ssd_norm_g, ssd_w_out, s5_log_dt, s5_a_re, s5_a_im, s5_b_re, s5_b_im, s5_c_re, s5_c_im, s5_d, s5_w_glu, ffn_w_gu, ffn_w_down, final_norm_g):
    depth = norm_mix_g.shape[0]
    n_mixers = 3
    h = x
    for i in range(depth):
        mixer, j = i % n_mixers, i // n_mixers
        if mixer == 0:
            h = _gla_layer(h, norm_mix_g[i], gla_w_in[j], gla_w_a2[j], gla_b_a[j],
                           gla_norm_g[j], gla_w_out[j])
        elif mixer == 1:
            h = _ssd_layer(h, norm_mix_g[i], ssd_w_in[j], ssd_conv_w[j], ssd_conv_b[j],
                           ssd_dt_bias[j], ssd_a_log[j], ssd_d[j], ssd_norm_g[j], ssd_w_out[j])
        else:
            h = _s5_layer(h, norm_mix_g[i], s5_log_dt[j], s5_a_re[j], s5_a_im[j], s5_b_re[j],
                          s5_b_im[j], s5_c_re[j], s5_c_im[j], s5_d[j], s5_w_glu[j])
        h = _ffn_layer(h, norm_ffn_g[i], ffn_w_gu[i], ffn_w_down[i],
                       final_norm_g if i == depth - 1 else None)
    return h
```

```python
import functools
import math

import jax
import jax.numpy as jnp
from jax import lax
from jax.experimental import pallas as pl
from jax.experimental.pallas import tpu as pltpu

F32 = jnp.float32
BF16 = jnp.bfloat16

EPS = 1e-6
CHUNK = 64
SUB = 2 * CHUNK
LANES = 128
SUBLANES = 8

GLA_HEADS = 4
GLA_DK = 128
GLA_DV = 256
GLA_RANK = 16
GLA_TAU = 16.0
SSD_HEADDIM = 64
SSD_GROUPS = 8
SSD_HPG = 4
SSD_DSTATE = 128
SSD_CONV = 4
S5_GROUP = 16
S5_STATE = 64
S5_GPB = LANES // S5_GROUP

VMEM_LIMIT = 56 * 1024 * 1024


def _dot(a, b):
    return jnp.dot(a, b, preferred_element_type=F32)


def _dot_nt(a, b):
    return lax.dot_general(a, b, (((1,), (1,)), ((), ())), preferred_element_type=F32)


def _dot_tn(a, b):
    return lax.dot_general(a, b, (((0,), (0,)), ((), ())), preferred_element_type=F32)


def _split3(x):
    h1 = x.astype(BF16)
    r1 = x - h1.astype(F32)
    h2 = r1.astype(BF16)
    r2 = r1 - h2.astype(F32)
    return h1, h2, r2.astype(BF16)


def _dot_exact_lhs(t, x):
    h1, h2, h3 = _split3(x)
    return _dot(t, h1) + _dot(t, h2) + _dot(t, h3)


def _rms(x, g):
    ms = jnp.mean(x * x, axis=-1, keepdims=True)
    return x * lax.rsqrt(ms + EPS) * g


def _sigmoid(x):
    return 1.0 / (1.0 + jnp.exp(-x))


def _softplus(x):
    return jnp.maximum(x, 0.0) + jnp.log1p(jnp.exp(-jnp.abs(x)))


def _const_spec(shape):
    nd = len(shape)
    return pl.BlockSpec(shape, lambda *_: (0,) * nd, pipeline_mode=pl.Buffered(1))


def _layer_spec(shape, layer):
    nd = len(shape)
    return pl.BlockSpec((None,) + tuple(shape), lambda *_: (layer,) + (0,) * nd,
                        pipeline_mode=pl.Buffered(1))


def _params(grid_rank=2):
    return pltpu.CompilerParams(
        dimension_semantics=("arbitrary",) * grid_rank,
        vmem_limit_bytes=VMEM_LIMIT)


def _ffn_kernel(h_ref, ng_ref, wgu_ref, wd_ref, fg_ref, o_ref, *, hidden, final):
    x = h_ref[0]
    xn = _rms(x, ng_ref[...]).astype(BF16)
    gu = _dot(xn, wgu_ref[...])
    g = gu[:, :hidden]
    u = gu[:, hidden:]
    a = (g * _sigmoid(g) * u).astype(BF16)
    out = x + _dot(a, wd_ref[...])
    if final:
        out = _rms(out, fg_ref[...])
    o_ref[0] = out


def _ffn_layer(h, norm_g, w_gu, w_down, layer, final_g, *, tm=512):
    bsz, seq, d = h.shape
    hidden = w_down.shape[1]
    tm = min(tm, seq)
    final = final_g is not None
    fg = (final_g if final else norm_g).reshape(1, d)
    kern = functools.partial(_ffn_kernel, hidden=hidden, final=final)
    return pl.pallas_call(
        kern,
        grid=(bsz, seq // tm),
        in_specs=[
            pl.BlockSpec((1, tm, d), lambda b, j: (b, j, 0)),
            _const_spec((1, d)),
            _layer_spec((d, 2 * hidden), layer),
            _layer_spec((hidden, d), layer),
            _const_spec((1, d)),
        ],
        out_specs=pl.BlockSpec((1, tm, d), lambda b, j: (b, j, 0)),
        out_shape=jax.ShapeDtypeStruct(h.shape, F32),
        compiler_params=_params(),
        name="swiglu_ffn",
    )(h, norm_g.reshape(1, d), w_gu, w_down, fg)


def _pipe_specs(bsz, seq, tm, d):
    nt = seq // tm
    n_tiles = bsz * nt

    def next_a(i):
        t = jnp.minimum(2 * i + 2, n_tiles - 1)
        return (t // nt, t % nt, 0)

    def pair(i):
        return (i // (nt // 2), i % (nt // 2), 0)

    in_next = pl.BlockSpec((1, tm, d), next_a)
    in_pair = pl.BlockSpec((1, 2 * tm, d), pair)
    out_pair = pl.BlockSpec((1, 2 * tm, d), pair)
    return nt, n_tiles // 2, in_next, in_pair, out_pair


def _interleave(*stages):
    live = [iter(st) for st in stages]
    while live:
        for st in list(live):
            try:
                next(st)
            except StopIteration:
                live.remove(st)


def _gla_stage1(x, ng_ref, w_ref, wa1_ref, wa2_ref, ba_ref, proj_o, la_o):
    nmain = proj_o.shape[1]
    hn = _rms(x, ng_ref[...]).astype(BF16)
    a_low = _dot(hn, wa1_ref[...]).astype(BF16)
    a = _dot(a_low, wa2_ref[...]) + ba_ref[...]
    la_o[...] = (jnp.minimum(a, 0.0) - jnp.log1p(jnp.exp(-jnp.abs(a)))) * (1.0 / GLA_TAU)
    yield
    cblk = 512
    for c in range(nmain // cblk):
        proj_o[:, c * cblk:(c + 1) * cblk] = _dot(hn, w_ref[:, c * cblk:(c + 1) * cblk])
        yield


def _gla_stage2(x, o_ref, out_rows, first_tile, proj_i, la_i, gated_s, state_s, og_ref, wo_ref, *, tm):
    qk = GLA_HEADS * GLA_DK
    vd = GLA_HEADS * GLA_DV
    ti = lax.broadcasted_iota(jnp.int32, (SUB, SUB), 0)
    si = lax.broadcasted_iota(jnp.int32, (SUB, SUB), 1)
    same = (ti < CHUNK) == (si < CHUNK)
    tri = jnp.where(same & (si <= ti), 1.0, 0.0).astype(BF16)
    code = jnp.where(same, jnp.where(si <= ti, 0, 1), jnp.where(si < ti, 2, 3))
    first = lax.broadcasted_iota(jnp.int32, (SUB, 1), 0) < CHUNK
    og = og_ref[...]

    for sub in range(tm // SUB):
        rows = slice(sub * SUB, (sub + 1) * SUB)
        lc = _dot_exact_lhs(tri, la_i[rows, :])
        lend0 = lc[CHUNK - 1:CHUNK, :]
        lend1 = lc[SUB - 1:SUB, :]
        lend = jnp.where(first, lend0, lend1)
        e_pos = jnp.exp(lc)
        e_neg = jnp.exp(-lc)
        e_st = jnp.exp(lend - lc)
        g0 = jnp.exp(lend0)
        g1 = jnp.exp(lend1)
        q = proj_i[rows, 0:qk] * (GLA_DK ** -0.5)
        k = proj_i[rows, qk:2 * qk]
        q_fwd = q * e_pos
        q_bwd = (q * e_neg).astype(BF16)
        k_fwd = (k * e_neg).astype(BF16)
        k_bwd = (k * e_pos).astype(BF16)
        k_st = k * e_st
        q_t = (q_fwd * jnp.where(first, 1.0, g0)).astype(BF16)
        k_t = (k_st * jnp.where(first, g1, 1.0)).astype(BF16)
        q_fwd = q_fwd.astype(BF16)
        k_st = k_st.astype(BF16)
        g_all = g0 * g1
        yield
        for hd in range(GLA_HEADS):
            ks = slice(hd * GLA_DK, (hd + 1) * GLA_DK)
            vs = slice(hd * GLA_DV, (hd + 1) * GLA_DV)
            s_past = _dot_nt(q_fwd[:, ks], k_fwd[:, ks])
            s_future = _dot_nt(q_bwd[:, ks], k_bwd[:, ks])
            s_cross = _dot_nt(q_fwd[:, ks], k_st[:, ks])
            p = jnp.where(code == 0, s_past,
                          jnp.where(code == 1, s_future,
                                    jnp.where(code == 2, s_cross, 0.0))).astype(BF16)
            v = proj_i[rows, 2 * qk + hd * GLA_DV:2 * qk + (hd + 1) * GLA_DV].astype(BF16)
            st = state_s[hd]
            if sub == 0:
                st = jnp.where(first_tile, 0.0, st)
            o = _dot(p, v) + _dot(q_t[:, ks], st.astype(BF16))
            g_col = jnp.broadcast_to(g_all[:, ks], (GLA_DK, GLA_DK)).T
            state_s[hd] = st * jnp.concatenate([g_col, g_col], axis=1) + _dot_tn(k_t[:, ks], v)
            ms = jnp.mean(o * o, axis=-1, keepdims=True)
            on = o * lax.rsqrt(ms + EPS) * og[:, vs]
            r = proj_i[rows, 2 * qk + vd + hd * GLA_DV:2 * qk + vd + (hd + 1) * GLA_DV]
            gated_s[rows, vs] = (on * (r * _sigmoid(r))).astype(BF16)
            yield
    o_ref[0, out_rows, :] = x + _dot(gated_s[...], wo_ref[...])


def _gla_kernel(hn_ref, hp_ref, ng_ref, w_ref, wa1_ref, wa2_ref, ba_ref, og_ref, wo_ref, o_ref,
                proj_a, proj_b, la_a, la_b, gat_a, gat_b, state_s, *, tm, nt):
    i = pl.program_id(0)
    stage1 = functools.partial(_gla_stage1, ng_ref=ng_ref, w_ref=w_ref, wa1_ref=wa1_ref,
                               wa2_ref=wa2_ref, ba_ref=ba_ref)
    stage2 = functools.partial(_gla_stage2, state_s=state_s, og_ref=og_ref, wo_ref=wo_ref, tm=tm)
    rows_a = slice(0, tm)
    rows_b = slice(tm, 2 * tm)

    @pl.when(i == 0)
    def _():
        _interleave(stage1(hp_ref[0, rows_a, :], proj_o=proj_a, la_o=la_a))

    first_a = (2 * i) % nt == 0
    _interleave(stage2(hp_ref[0, rows_a, :], o_ref, rows_a, first_a, proj_a, la_a, gat_a),
                stage1(hp_ref[0, rows_b, :], proj_o=proj_b, la_o=la_b))
    _interleave(stage2(hp_ref[0, rows_b, :], o_ref, rows_b, False, proj_b, la_b, gat_b),
                stage1(hn_ref[0], proj_o=proj_a, la_o=la_a))


def _gla_layer(h, norm_g, w_in, w_in_b, layer, w_a2, b_a, out_g, w_out_b, *, tm=256):
    bsz, seq, d = h.shape
    qk = GLA_HEADS * GLA_DK
    vd = GLA_HEADS * GLA_DV
    nmain = 2 * qk + 2 * vd
    tm = min(tm, seq // 2)
    nt, steps, in_next, in_pair, out_pair = _pipe_specs(bsz, seq, tm, d)
    w_a1 = jnp.pad(w_in[:, nmain:], ((0, 0), (0, LANES - GLA_RANK))).astype(BF16)
    w_a2p = jnp.pad(w_a2, ((0, LANES - GLA_RANK), (0, 0))).astype(BF16)
    kern = functools.partial(_gla_kernel, tm=tm, nt=nt)
    return pl.pallas_call(
        kern,
        grid=(steps,),
        in_specs=[
            in_next,
            in_pair,
            _const_spec((1, d)),
            _layer_spec(w_in.shape, layer),
            _const_spec((d, LANES)),
            _const_spec((LANES, qk)),
            _const_spec((1, qk)),
            _const_spec((1, vd)),
            _layer_spec((vd, d), layer),
        ],
        out_specs=out_pair,
        out_shape=jax.ShapeDtypeStruct(h.shape, F32),
        scratch_shapes=[
            pltpu.VMEM((tm, nmain), F32), pltpu.VMEM((tm, nmain), F32),
            pltpu.VMEM((tm, qk), F32), pltpu.VMEM((tm, qk), F32),
            pltpu.VMEM((tm, vd), BF16), pltpu.VMEM((tm, vd), BF16),
            pltpu.VMEM((GLA_HEADS, GLA_DK, GLA_DV), F32),
        ],
        compiler_params=_params(1),
        name="gla_mixer",
    )(h, h, norm_g.reshape(1, d), w_in_b, w_a1, w_a2p, b_a.reshape(1, qk),
      out_g.reshape(1, vd), w_out_b)


def _pack3(x):
    h1, h2, h3 = _split3(x)
    lane = lax.broadcasted_iota(jnp.int32, x.shape, 1)
    zero = jnp.zeros_like(h1)
    return jnp.where(lane < 32, h1, jnp.where(lane < 64, h2, jnp.where(lane < 96, h3, zero)))


def _ssd_stage1(x, first_tile, ng_ref, w_ref, wdt_ref, cw_ref, cb_ref, dtb_ref, alog_ref,
                ext_s, tail_s, xbc_o, gz_o, dt_o, da_o, *, tm):
    dinner = gz_o.shape[1]
    cdim = xbc_o.shape[1]
    hn = _rms(x, ng_ref[...]).astype(BF16)
    dt = _softplus(_dot(hn, wdt_ref[...]) + dtb_ref[...])
    dt_o[...] = dt
    da_o[...] = dt * (-jnp.exp(alog_ref[...]))
    yield
    cblk = 512
    for c in range(dinner // cblk):
        cs = slice(c * cblk, (c + 1) * cblk)
        z = _dot(hn, w_ref[:, cs])
        gz_o[:, cs] = z * _sigmoid(z)
        yield
    for c in range(cdim // cblk):
        cs = slice(c * cblk, (c + 1) * cblk)
        ext_s[SUBLANES:SUBLANES + tm, cs] = _dot(hn, w_ref[:, dinner + c * cblk:dinner + (c + 1) * cblk])
        ext_s[0:SUBLANES, cs] = jnp.where(first_tile, 0.0, tail_s[:, cs])
        acc = cb_ref[:, cs] + cw_ref[SSD_CONV - 1:SSD_CONV, cs] * ext_s[SUBLANES:SUBLANES + tm, cs]
        for kk in range(SSD_CONV - 1):
            off = SUBLANES - (SSD_CONV - 1) + kk
            acc = acc + cw_ref[kk:kk + 1, cs] * ext_s[off:off + tm, cs]
        xbc_o[:, cs] = acc * _sigmoid(acc)
        tail_s[:, cs] = ext_s[tm:tm + SUBLANES, cs]
        yield


def _ssd_stage2(x, o_ref, out_rows, first_tile, xbc_i, gz_i, dt_i, da_i, y_s, ht_s, dsk_ref, og_ref,
                wo_ref, ex_ref, *, tm):
    dinner = gz_i.shape[1]
    gn = SSD_GROUPS * SSD_DSTATE
    gw = SSD_HPG * SSD_HEADDIM
    ti = lax.broadcasted_iota(jnp.int32, (SUB, SUB), 0)
    si = lax.broadcasted_iota(jnp.int32, (SUB, SUB), 1)
    tri = jnp.where(si <= ti, 1.0, 0.0).astype(BF16)
    visible = (si < CHUNK) | (ti >= CHUNK)
    head_of_lane = lax.broadcasted_iota(jnp.int32, (SUB, gw), 1) // SSD_HEADDIM
    og = og_ref[...]
    dsk = dsk_ref[...]
    expand = ex_ref[...]
    neg_log2e = -1.4426950408889634

    for sub in range(tm // SUB):
        rows = slice(sub * SUB, (sub + 1) * SUB)
        dts = dt_i[rows, :]
        cum = _dot_exact_lhs(tri, da_i[rows, :])
        cum_end = cum[SUB - 1:SUB, :]
        dec_e = _dot(_pack3(jnp.exp(cum)), expand)
        dec_w = _dot(_pack3(dts * jnp.exp(cum_end - cum)), expand)
        cum_t = cum.T
        dt_t = dts.T
        xs = xbc_i[rows, 0:dinner]
        xs_b = xs.astype(BF16)
        xw = (xs * dec_w).astype(BF16)
        yield
        for g in range(SSD_GROUPS):
            gs = slice(g * gw, (g + 1) * gw)
            bm = xbc_i[rows, dinner + g * SSD_DSTATE:dinner + (g + 1) * SSD_DSTATE].astype(BF16)
            cm = xbc_i[rows, dinner + gn + g * SSD_DSTATE:dinner + gn + (g + 1) * SSD_DSTATE].astype(BF16)
            cb = jnp.where(visible, _dot_nt(cm, bm), 0.0)
            ht = ht_s[g]
            if sub == 0:
                ht = jnp.where(first_tile, 0.0, ht)
            y_state = _dot(cm, ht.astype(BF16)) * dec_e[:, gs]
            mixes = []
            diag = []
            for jj in range(SSD_HPG):
                hd = g * SSD_HPG + jj
                diff = cum[:, hd:hd + 1] - cum_t[hd:hd + 1, :]
                mix = cb * jnp.exp2(jnp.abs(diff) * neg_log2e) * dt_t[hd:hd + 1, :]
                mixes.append(mix.astype(BF16))
                diag.append(jnp.where(head_of_lane == jj, xs_b[:, gs], jnp.zeros_like(xs_b[:, gs])))
            y = _dot(jnp.concatenate(mixes, axis=1), jnp.concatenate(diag, axis=0))
            y = y + y_state + dsk[:, gs] * xs[:, gs]
            ht_s[g] = ht * dec_e[SUB - 1:SUB, gs] + _dot_tn(bm, xw[:, gs])
            y = y * gz_i[rows, gs]
            ms = jnp.mean(y * y, axis=-1, keepdims=True)
            y_s[rows, gs] = (y * lax.rsqrt(ms + EPS) * og[:, gs]).astype(BF16)
            yield
    o_ref[0, out_rows, :] = x + _dot(y_s[...], wo_ref[...])


def _ssd_kernel(hn_ref, hp_ref, ng_ref, w_ref, wdt_ref, cw_ref, cb_ref, dtb_ref, alog_ref,
                dsk_ref, og_ref, wo_ref, ex_ref, o_ref,
                ext_s, tail_s, xbc_a, xbc_b, gz_a, gz_b, dt_a, dt_b, da_a, da_b, y_a, y_b, ht_s,
                *, tm, nt):
    i = pl.program_id(0)
    stage1 = functools.partial(_ssd_stage1, ng_ref=ng_ref, w_ref=w_ref, wdt_ref=wdt_ref,
                               cw_ref=cw_ref, cb_ref=cb_ref, dtb_ref=dtb_ref, alog_ref=alog_ref,
                               ext_s=ext_s, tail_s=tail_s, tm=tm)
    stage2 = functools.partial(_ssd_stage2, ht_s=ht_s, dsk_ref=dsk_ref, og_ref=og_ref,
                               wo_ref=wo_ref, ex_ref=ex_ref, tm=tm)
    rows_a = slice(0, tm)
    rows_b = slice(tm, 2 * tm)

    @pl.when(i == 0)
    def _():
        _interleave(stage1(hp_ref[0, rows_a, :], True, xbc_o=xbc_a, gz_o=gz_a, dt_o=dt_a, da_o=da_a))

    first_a = (2 * i) % nt == 0
    first_next = (2 * i + 2) % nt == 0
    _interleave(stage2(hp_ref[0, rows_a, :], o_ref, rows_a, first_a, xbc_a, gz_a, dt_a, da_a, y_a),
                stage1(hp_ref[0, rows_b, :], False, xbc_o=xbc_b, gz_o=gz_b, dt_o=dt_b, da_o=da_b))
    _interleave(stage2(hp_ref[0, rows_b, :], o_ref, rows_b, False, xbc_b, gz_b, dt_b, da_b, y_b),
                stage1(hn_ref[0], first_next, xbc_o=xbc_a, gz_o=gz_a, dt_o=dt_a, da_o=da_a))


def _ssd_layer(h, norm_g, w_in, conv_w, conv_b, dt_bias, a_log, d_skip, out_g, w_out, *, tm=256):
    bsz, seq, d = h.shape
    heads = SSD_GROUPS * SSD_HPG
    dinner = heads * SSD_HEADDIM
    gn = SSD_GROUPS * SSD_DSTATE
    cdim = dinner + 2 * gn
    tm = min(tm, seq // 2)
    nt, steps, in_next, in_pair, out_pair = _pipe_specs(bsz, seq, tm, d)

    def rep3(v):
        pad = [(0, 0)] * (v.ndim - 1) + [(0, LANES - 3 * heads)]
        return jnp.pad(jnp.concatenate([v, v, v], axis=-1), pad)

    w_dt = rep3(w_in[:, dinner + cdim:]).astype(BF16)
    lane = jnp.arange(LANES)[:, None]
    col = jnp.arange(dinner)[None, :]
    expand = ((lane < 3 * heads) & ((lane % heads) == (col // SSD_HEADDIM))).astype(BF16)
    kern = functools.partial(_ssd_kernel, tm=tm, nt=nt)
    slot = lambda shape, dt: [pltpu.VMEM(shape, dt), pltpu.VMEM(shape, dt)]
    return pl.pallas_call(
        kern,
        grid=(steps,),
        in_specs=[
            in_next,
            in_pair,
            _const_spec((1, d)),
            _const_spec(w_in.shape),
            _const_spec((d, LANES)),
            _const_spec((SSD_CONV, cdim)),
            _const_spec((1, cdim)),
            _const_spec((1, LANES)),
            _const_spec((1, LANES)),
            _const_spec((1, dinner)),
            _const_spec((1, dinner)),
            _const_spec((dinner, d)),
            _const_spec((LANES, dinner)),
        ],
        out_specs=out_pair,
        out_shape=jax.ShapeDtypeStruct(h.shape, F32),
        scratch_shapes=[
            pltpu.VMEM((tm + SUBLANES, cdim), F32),
            pltpu.VMEM((SUBLANES, cdim), F32),
            *slot((tm, cdim), F32),
            *slot((tm, dinner), F32),
            *slot((tm, LANES), F32),
            *slot((tm, LANES), F32),
            *slot((tm, dinner), BF16),
            pltpu.VMEM((SSD_GROUPS, SSD_DSTATE, SSD_HPG * SSD_HEADDIM), F32),
        ],
        compiler_params=_params(1),
        name="ssd_mixer",
    )(h, h, norm_g.reshape(1, d), w_in.astype(BF16), w_dt, conv_w, conv_b.reshape(1, cdim),
      rep3(dt_bias).reshape(1, LANES), rep3(a_log).reshape(1, LANES),
      jnp.repeat(d_skip, SSD_HEADDIM).reshape(1, dinner), out_g.reshape(1, dinner),
      w_out.astype(BF16), expand)


def _s5_kernel(h_ref, ng_ref, wb_ref, wc_ref, ta_ref, tp_ref, dsk_ref, wg_ref, o_ref,
               xr_s, xi_s, xs_s, y_s, car_s, *, tm):
    d = h_ref.shape[2]
    nblk = d // LANES
    sw = S5_GPB * S5_STATE
    tt = tm // SUBLANES

    @pl.when(pl.program_id(1) == 0)
    def _():
        car_s[...] = jnp.zeros_like(car_s)

    x = h_ref[0]
    u = _rms(x, ng_ref[...])
    up = jnp.swapaxes(u.reshape(SUBLANES, tt, d), 0, 1).reshape(tm, d).astype(BF16)
    row0 = lax.broadcasted_iota(jnp.int32, (SUBLANES, sw), 0) == 0

    for blk in range(nblk):
        slot = blk % 2
        bu = _dot(up[:, blk * LANES:(blk + 1) * LANES], wb_ref[blk])
        ar = ta_ref[blk, 0]
        ai = ta_ref[blk, 1]
        xr = bu[0:SUBLANES, :sw]
        xi = bu[0:SUBLANES, sw:]
        xr_s[slot, 0:SUBLANES, :] = xr
        xi_s[slot, 0:SUBLANES, :] = xi
        for t in range(1, tt):
            rows = slice(t * SUBLANES, (t + 1) * SUBLANES)
            xr, xi = bu[rows, :sw] + ar * xr - ai * xi, bu[rows, sw:] + ar * xi + ai * xr
            xr_s[slot, rows, :] = xr
            xi_s[slot, rows, :] = xi
        er = jnp.where(row0, car_s[blk, 0], pltpu.roll(xr, 1, 0))
        ei = jnp.where(row0, car_s[blk, 1], pltpu.roll(xi, 1, 0))
        for n, sh in enumerate((1, 2, 4)):
            mr, mi = ta_ref[blk, 2 + 2 * n], ta_ref[blk, 3 + 2 * n]
            rr = pltpu.roll(er, sh, 0)
            ri = pltpu.roll(ei, sh, 0)
            er, ei = er + mr * rr - mi * ri, ei + mr * ri + mi * rr
        lr = tp_ref[blk, 0, tt - 1:tt, :]
        li = tp_ref[blk, 1, tt - 1:tt, :]
        nr = xr + lr * er - li * ei
        ni = xi + lr * ei + li * er
        car_s[blk, 0] = jnp.broadcast_to(nr[SUBLANES - 1:SUBLANES, :], (SUBLANES, sw))
        car_s[blk, 1] = jnp.broadcast_to(ni[SUBLANES - 1:SUBLANES, :], (SUBLANES, sw))
        for t in range(0, tt, 2):
            outs_r, outs_i = [], []
            for t2 in (t, t + 1):
                rows = slice(t2 * SUBLANES, (t2 + 1) * SUBLANES)
                pr = tp_ref[blk, 0, t2:t2 + 1, :]
                pi = tp_ref[blk, 1, t2:t2 + 1, :]
                outs_r.append(xr_s[slot, rows, :] + pr * er - pi * ei)
                outs_i.append(xi_s[slot, rows, :] + pr * ei + pi * er)
            rows2 = slice(t * SUBLANES, (t + 2) * SUBLANES)
            xs_s[slot, rows2, :sw] = jnp.concatenate(outs_r, axis=0).astype(BF16)
            xs_s[slot, rows2, sw:] = jnp.concatenate(outs_i, axis=0).astype(BF16)
        y_s[:, blk * LANES:(blk + 1) * LANES] = _dot(xs_s[slot], wc_ref[blk])

    y = jnp.swapaxes(y_s[...].reshape(tt, SUBLANES, d), 0, 1).reshape(tm, d) + dsk_ref[...] * u
    c0 = math.sqrt(2.0 / math.pi)
    y = y * (0.5 * (1.0 + jnp.tanh(c0 * (y + 0.044715 * (y * y * y)))))
    vg = _dot(y.astype(BF16), wg_ref[...])
    o_ref[0] = x + vg[:, :d] * _sigmoid(vg[:, d:])


def _s5_tables(log_dt, a_re, a_im, b_re, b_im, c_re, c_im, tt):
    groups, nstate = a_re.shape
    nblk = groups // S5_GPB
    sw = S5_GPB * S5_STATE
    step = jnp.exp(log_dt.astype(F32))[:, None]
    a_re = a_re.astype(F32)
    a_im = a_im.astype(F32)
    mag = jnp.exp(step * a_re)
    abar_re = mag * jnp.cos(step * a_im)
    abar_im = mag * jnp.sin(step * a_im)
    den = a_re * a_re + a_im * a_im
    num_re = abar_re - 1.0
    num_im = abar_im
    f_re = (num_re * a_re + num_im * a_im) / den
    f_im = (num_im * a_re - num_re * a_im) / den
    b_re = b_re.astype(F32)
    b_im = b_im.astype(F32)
    bb_re = f_re[..., None] * b_re - f_im[..., None] * b_im
    bb_im = f_re[..., None] * b_im + f_im[..., None] * b_re
    eye = jnp.eye(S5_GPB, dtype=F32)

    def in_mat(bb):
        t = bb.reshape(nblk, S5_GPB, nstate, S5_GROUP)
        m = jnp.einsum('bgpc,gh->bgchp', t, eye)
        return m.reshape(nblk, LANES, sw)

    def out_mat(cc):
        t = cc.reshape(nblk, S5_GPB, S5_GROUP, nstate)
        m = jnp.einsum('bgcp,gh->bgphc', t, eye)
        return m.reshape(nblk, sw, LANES)

    w_b = jnp.concatenate([in_mat(bb_re), in_mat(bb_im)], axis=2).astype(BF16)
    w_c = jnp.concatenate([out_mat(c_re.astype(F32)), -out_mat(c_im.astype(F32))], axis=1).astype(BF16)

    def power(k):
        m = jnp.exp(k * step * a_re)
        return m * jnp.cos(k * step * a_im), m * jnp.sin(k * step * a_im)

    def lay(t):
        return t.reshape(t.shape[0], nblk, sw).transpose(1, 0, 2)

    row = jnp.arange(SUBLANES, dtype=F32)[:, None, None]
    one = jnp.ones((SUBLANES, 1, 1), F32)
    pr, pi = power(one)
    tabs = [lay(pr), lay(pi)]
    for sh in (1, 2, 4):
        pr, pi = power(one * float(sh * tt))
        keep = (row >= sh).astype(F32)
        tabs += [lay(keep * pr), lay(keep * pi)]
    tab_a = jnp.stack(tabs, axis=1)
    tau = (jnp.arange(tt, dtype=F32) + 1.0)[:, None, None]
    pr, pi = power(tau)
    tab_p = jnp.stack([lay(pr), lay(pi)], axis=1)
    return w_b, w_c, tab_a, tab_p


def _s5_layer(h, norm_g, log_dt, a_re, a_im, b_re, b_im, c_re, c_im, d_skip, w_glu, *, tm=512):
    bsz, seq, d = h.shape
    nblk = d // LANES
    sw = S5_GPB * S5_STATE
    tm = min(tm, seq)
    tt = tm // SUBLANES
    w_b, w_c, tab_a, tab_p = _s5_tables(log_dt, a_re, a_im, b_re, b_im, c_re, c_im, tt)
    kern = functools.partial(_s5_kernel, tm=tm)
    return pl.pallas_call(
        kern,
        grid=(bsz, seq // tm),
        in_specs=[
            pl.BlockSpec((1, tm, d), lambda b, j: (b, j, 0)),
            _const_spec((1, d)),
            _const_spec((nblk, LANES, 2 * sw)),
            _const_spec((nblk, 2 * sw, LANES)),
            _const_spec((nblk, 8, SUBLANES, sw)),
            _const_spec((nblk, 2, tt, sw)),
            _const_spec((1, d)),
            _const_spec((d, 2 * d)),
        ],
        out_specs=pl.BlockSpec((1, tm, d), lambda b, j: (b, j, 0)),
        out_shape=jax.ShapeDtypeStruct(h.shape, F32),
        scratch_shapes=[
            pltpu.VMEM((2, tm, sw), F32),
            pltpu.VMEM((2, tm, sw), F32),
            pltpu.VMEM((2, tm, 2 * sw), BF16),
            pltpu.VMEM((tm, d), F32),
            pltpu.VMEM((nblk, 2, SUBLANES, sw), F32),
        ],
        compiler_params=_params(),
        name="s5_mixer",
    )(h, norm_g.reshape(1, d), w_b, w_c, tab_a, tab_p, d_skip.astype(F32).reshape(1, d),
      w_glu.astype(BF16))


def kernel(x, norm_mix_g, norm_ffn_g, gla_w_in, gla_w_a2, gla_b_a, gla_norm_g, gla_w_out, ssd_w_in, ssd_conv_w, ssd_conv_b, ssd_dt_bias, ssd_a_log, ssd_d, ssd_norm_g, ssd_w_out, s5_log_dt, s5_a_re, s5_a_im, s5_b_re, s5_b_im, s5_c_re, s5_c_im, s5_d, s5_w_glu, ffn_w_gu, ffn_w_down, final_norm_g):
    depth = norm_mix_g.shape[0]
    n_mixers = 3
    gla_w_in_b = gla_w_in.astype(BF16)
    gla_w_out_b = gla_w_out.astype(BF16)
    ffn_w_gu_b = ffn_w_gu.astype(BF16)
    ffn_w_down_b = ffn_w_down.astype(BF16)
    h = x
    for i in range(depth):
        mixer, j = i % n_mixers, i // n_mixers
        if mixer == 0:
            h = _gla_layer(h, norm_mix_g[i], gla_w_in[j], gla_w_in_b, j, gla_w_a2[j], gla_b_a[j],
                           gla_norm_g[j], gla_w_out_b)
        elif mixer == 1:
            h = _ssd_layer(h, norm_mix_g[i], ssd_w_in[j], ssd_conv_w[j], ssd_conv_b[j],
                           ssd_dt_bias[j], ssd_a_log[j], ssd_d[j], ssd_norm_g[j], ssd_w_out[j])
        else:
            h = _s5_layer(h, norm_mix_g[i], s5_log_dt[j], s5_a_re[j], s5_a_im[j], s5_b_re[j],
                          s5_b_im[j], s5_c_re[j], s5_c_im[j], s5_d[j], s5_w_glu[j])
        h = _ffn_layer(h, norm_ffn_g[i], ffn_w_gu_b, ffn_w_down_b, i,
                       final_norm_g if i == depth - 1 else None)
    return h
```
